```python
import jax, jax.numpy as jnp
from jax import lax
import numpy as np

D_MODEL = 1024
BATCH = 8
SEQ = 2048
DEPTH = 4
DEC_BATCH = 128
DEC_SEQ = 8
PAST_LEN = 2048
PAGE_SIZE = 128

HEAD_DIM = 64
MIX_WIDTH = D_MODEL
FOX_WIDTH = MIX_WIDTH // 2
DSA_WIDTH = MIX_WIDTH - FOX_WIDTH
FOX_HEADS = FOX_WIDTH // HEAD_DIM
DSA_HEADS = DSA_WIDTH // HEAD_DIM
DSA_KV_HEADS = 2
DSA_GROUP = DSA_HEADS // DSA_KV_HEADS
DSA_KV_WIDTH = DSA_KV_HEADS * HEAD_DIM
IDX_HEADS = 4
IDX_DIM = 64
MAX_SELECT = 256
N_META = 16
Q_BLOCK = 128
N_EXPERTS = 16
N_EXPERT_GROUPS = 4
EXPERTS_PER_GROUP = N_EXPERTS // N_EXPERT_GROUPS
TOP_K = 2
TOPK_GROUPS = 1
D_EXPERT = D_MODEL // 4
LN_EPS = 1e-5
DEEPNORM_ALPHA = (2.0 * DEPTH) ** 0.25
DEEPNORM_BETA = (8.0 * DEPTH) ** -0.25
FORGET_BIAS_INIT = 3.0

SPLIT_NAMES = ("fox_q", "fox_k", "fox_v", "fox_f", "dsa_q", "dsa_k", "dsa_v",
               "idx_q", "idx_k", "idx_w")
SPLIT_SIZES = (FOX_WIDTH, FOX_WIDTH, FOX_WIDTH, FOX_HEADS, DSA_WIDTH, DSA_KV_WIDTH,
               DSA_KV_WIDTH, IDX_HEADS * IDX_DIM, IDX_DIM, IDX_HEADS)
N_IN = sum(SPLIT_SIZES)

kernel_name = "hymba_fox_dsa_moe_step"


def _layer_norm(x, g, b):
    xf = x.astype(jnp.float32)
    mu = jnp.mean(xf, axis=-1, keepdims=True)
    xc = xf - mu
    var = jnp.mean(xc * xc, axis=-1, keepdims=True)
    y = xc * lax.rsqrt(var + LN_EPS) * g.astype(jnp.float32) + b.astype(jnp.float32)
    return y.astype(x.dtype)


def _split_columns(h):
    pieces = []
    off = 0
    for size in SPLIT_SIZES:
        pieces.append(h[..., off:off + size])
        off += size
    return pieces


def _project(x, w_in, b_forget):
    h = x @ w_in
    fq, fk, fv, fg, dq, dk, dv, iq, ik, iw = _split_columns(h)
    lead = h.shape[:-1]
    fq = fq.reshape(lead + (FOX_HEADS, HEAD_DIM))
    fk = fk.reshape(lead + (FOX_HEADS, HEAD_DIM))
    fv = fv.reshape(lead + (FOX_HEADS, HEAD_DIM))
    logf = jax.nn.log_sigmoid(fg.astype(jnp.float32) + b_forget.astype(jnp.float32))
    dq = dq.reshape(lead + (DSA_HEADS, HEAD_DIM))
    dk = dk.reshape(lead + (DSA_KV_HEADS, HEAD_DIM))
    dv = dv.reshape(lead + (DSA_KV_HEADS, HEAD_DIM))
    iq = iq.reshape(lead + (IDX_HEADS, IDX_DIM))
    iw = iw.astype(jnp.float32) * (IDX_HEADS ** -0.5)
    return fq, fk, fv, logf, dq, dk, dv, iq, ik, iw


def _alibi_slopes():
    h = jnp.arange(1, DSA_HEADS + 1, dtype=jnp.float32)
    return jnp.exp2(-(8.0 / DSA_HEADS) * h).reshape(DSA_KV_HEADS, DSA_GROUP)


def _fox_attend(q, cq, q_pos, k, v, ck, k_pos):
    s = jnp.einsum('bqhd,bkhd->bhqk', q, k).astype(jnp.float32) * (HEAD_DIM ** -0.5)
    s = s + jnp.swapaxes(cq, 1, 2)[..., :, None] - jnp.swapaxes(ck, 1, 2)[..., None, :]
    causal = k_pos[None, :] <= q_pos[:, None]
    s = jnp.where(causal, s, -jnp.inf)
    p = jax.nn.softmax(s, axis=-1).astype(v.dtype)
    return jnp.einsum('bhqk,bkhd->bqhd', p, v)


def _dsa_attend(q, iq, iw, q_pos, k, v, ik, k_pos, n_sel):
    b, tq = q.shape[0], q.shape[1]
    dots = jnp.einsum('bqhe,bke->bqhk', iq, ik).astype(jnp.float32) * (IDX_DIM ** -0.5)
    score = jnp.einsum('bqh,bqhk->bqk', iw, jax.nn.relu(dots))
    causal = k_pos[None, :] <= q_pos[:, None]
    score = jnp.where(causal[None], score, -jnp.inf)
    _, sel = lax.top_k(score, n_sel)
    gather = jax.vmap(lambda rows, idx: rows[idx])
    k_sel = gather(k, sel)
    v_sel = gather(v, sel)
    dist = (q_pos[None, :, None] - k_pos[sel]).astype(jnp.float32)
    qg = q.reshape(b, tq, DSA_KV_HEADS, DSA_GROUP, HEAD_DIM)
    s = jnp.einsum('bqgrd,bqngd->bqgrn', qg, k_sel).astype(jnp.float32) * (HEAD_DIM ** -0.5)
    s = s - _alibi_slopes()[None, None, :, :, None] * dist[:, :, None, None, :]
    s = jnp.where((dist >= 0)[:, :, None, None, :], s, -jnp.inf)
    p = jax.nn.softmax(s, axis=-1).astype(v.dtype)
    o = jnp.einsum('bqgrn,bqngd->bqgrd', p, v_sel)
    return o.reshape(b, tq, DSA_WIDTH)


def _mix(qside, kside, n_sel):
    fq, cq, dq, iq, iw, q_pos = qside
    fk, fv, ck, dk, dv, ik, k_pos = kside
    b, tq = fq.shape[0], fq.shape[1]
    fo = _fox_attend(fq, cq, q_pos, fk, fv, ck, k_pos).reshape(b, tq, FOX_WIDTH)
    do = _dsa_attend(dq, iq, iw, q_pos, dk, dv, ik, k_pos, n_sel)
    return jnp.concatenate([fo, do], axis=-1)


def _prompt_mixer(x, w_in, b_forget, w_out):
    b, L, _ = x.shape
    fq, fk, fv, logf, dq, dk, dv, iq, ik, iw = _project(x, w_in, b_forget)
    c = jnp.cumsum(logf, axis=1)
    pos = jnp.arange(L, dtype=jnp.int32)
    n_sel = min(MAX_SELECT, (L - N_META) // 4)
    M = N_META
    meta_out = _mix((fq[:, :M], c[:, :M], dq[:, :M], iq[:, :M], iw[:, :M], pos[:M]),
                    (fk[:, :M], fv[:, :M], c[:, :M], dk[:, :M], dv[:, :M], ik[:, :M], pos[:M]),
                    min(n_sel, M))
    nb = (L - M) // Q_BLOCK

    def to_blocks(a):
        a = a[:, M:]
        return jnp.moveaxis(a.reshape((b, nb, Q_BLOCK) + a.shape[2:]), 1, 0)

    q_blocks = (to_blocks(fq), to_blocks(c), to_blocks(dq), to_blocks(iq), to_blocks(iw),
                pos[M:].reshape(nb, Q_BLOCK))
    kside = (fk, fv, c, dk, dv, ik, pos)
    blk = lax.map(lambda qs: _mix(qs, kside, n_sel), q_blocks)
    real_out = jnp.moveaxis(blk, 0, 1).reshape(b, L - M, MIX_WIDTH)
    out = jnp.concatenate([meta_out, real_out], axis=1) @ w_out
    return out, (fk, fv, logf, dk, dv, ik)


def _sample_mixer(x, pool_fk, pool_fv, pool_logf, pool_dk, pool_dv, pool_ik, l,
                  page_table, w_in, b_forget, w_out):
    bd, t, _ = x.shape
    P = page_table.shape[1] * PAGE_SIZE

    def past(pool):
        g = pool[l, page_table]
        return g.reshape((bd, P) + pool.shape[3:])

    fq, fk, fv, logf, dq, dk, dv, iq, ik, iw = _project(x, w_in, b_forget)
    fk_all = jnp.concatenate([past(pool_fk), fk], axis=1)
    fv_all = jnp.concatenate([past(pool_fv), fv], axis=1)
    logf_all = jnp.concatenate([past(pool_logf).astype(jnp.float32), logf], axis=1)
    c = jnp.cumsum(logf_all, axis=1)
    dk_all = jnp.concatenate([past(pool_dk), dk], axis=1)
    dv_all = jnp.concatenate([past(pool_dv), dv], axis=1)
    ik_all = jnp.concatenate([past(pool_ik), ik], axis=1)
    pos_all = jnp.arange(P + t, dtype=jnp.int32)
    n_sel = min(MAX_SELECT, (P + t) // 4)
    out = _mix((fq, c[:, P:], dq, iq, iw, pos_all[P:]),
               (fk_all, fv_all, c, dk_all, dv_all, ik_all, pos_all), n_sel) @ w_out
    return out, (fk, fv, logf, dk, dv, ik)


def _moe(x, w_router, b_router, w_gate, w_up, w_down):
    shp = x.shape
    t = x.reshape(-1, shp[-1])
    s = jax.nn.sigmoid((t @ w_router).astype(jnp.float32))
    sb = s + b_router.astype(jnp.float32)
    grp = sb.reshape(-1, N_EXPERT_GROUPS, EXPERTS_PER_GROUP)
    grp_score = lax.top_k(grp, TOP_K)[0].sum(-1)
    _, gi = lax.top_k(grp_score, TOPK_GROUPS)
    gmask = jax.nn.one_hot(gi, N_EXPERT_GROUPS, dtype=jnp.float32).sum(1) > 0
    emask = jnp.repeat(gmask, EXPERTS_PER_GROUP, axis=1)
    _, ei = lax.top_k(jnp.where(emask, sb, -jnp.inf), TOP_K)
    g = jnp.take_along_axis(s, ei, axis=1)
    g = g / jnp.sum(g, axis=-1, keepdims=True)
    gates = jnp.einsum('tk,tke->te', g, jax.nn.one_hot(ei, N_EXPERTS, dtype=jnp.float32))
    h = jax.nn.silu(jnp.einsum('td,edf->tef', t, w_gate)) * jnp.einsum('td,edf->tef', t, w_up)
    y = jnp.einsum('tef,efd->td', h * gates[..., None].astype(h.dtype), w_down)
    return y.reshape(shp)


def setup_inputs(seed: int = 0) -> dict:
    key = jax.random.key(seed)
    ks = jax.random.split(key, 24)
    f32 = jnp.float32
    n_pages = PAST_LEN // PAGE_SIZE
    n_used = DEC_BATCH * n_pages
    n_pool = n_used + max(1, n_used // 4)
    nrm = lambda k, shape: jax.random.normal(k, shape, f32)
    col_scale = jnp.concatenate([
        jnp.full((sz,), DEEPNORM_BETA if name in ("fox_v", "dsa_v") else 1.0, f32)
        for name, sz in zip(SPLIT_NAMES, SPLIT_SIZES)])
    page_table = jax.random.permutation(ks[9], n_pool)[:n_used].reshape(DEC_BATCH, n_pages).astype(jnp.int32)
    return {
        "x_prompt": nrm(ks[0], (BATCH, SEQ, D_MODEL)),
        "x_sample": nrm(ks[1], (DEC_BATCH, DEC_SEQ, D_MODEL)),
        "cache_fox_k": nrm(ks[2], (DEPTH, n_pool, PAGE_SIZE, FOX_HEADS, HEAD_DIM)),
        "cache_fox_v": nrm(ks[3], (DEPTH, n_pool, PAGE_SIZE, FOX_HEADS, HEAD_DIM)),
        "cache_fox_logf": jax.nn.log_sigmoid(FORGET_BIAS_INIT + 0.5 * nrm(ks[4], (DEPTH, n_pool, PAGE_SIZE, FOX_HEADS))),
        "cache_dsa_k": nrm(ks[5], (DEPTH, n_pool, PAGE_SIZE, DSA_KV_HEADS, HEAD_DIM)),
        "cache_dsa_v": nrm(ks[6], (DEPTH, n_pool, PAGE_SIZE, DSA_KV_HEADS, HEAD_DIM)),
        "cache_idx_k": nrm(ks[7], (DEPTH, n_pool, PAGE_SIZE, IDX_DIM)),
        "page_table": page_table,
        "meta_tokens": nrm(ks[8], (N_META, D_MODEL)),
        "w_in": nrm(ks[10], (DEPTH, D_MODEL, N_IN)) * (D_MODEL ** -0.5) * col_scale,
        "b_forget": FORGET_BIAS_INIT + 0.5 * nrm(ks[11], (DEPTH, FOX_HEADS)),
        "w_out": nrm(ks[12], (DEPTH, MIX_WIDTH, D_MODEL)) * (MIX_WIDTH ** -0.5) * DEEPNORM_BETA,
        "ln1_g": 1.0 + 0.05 * nrm(ks[13], (DEPTH, D_MODEL)),
        "ln1_b": 0.02 * nrm(ks[14], (DEPTH, D_MODEL)),
        "ln2_g": 1.0 + 0.05 * nrm(ks[15], (DEPTH, D_MODEL)),
        "ln2_b": 0.02 * nrm(ks[16], (DEPTH, D_MODEL)),
        "w_router": nrm(ks[17], (D_MODEL, N_EXPERTS)) * (D_MODEL ** -0.5),
        "b_router": 0.01 * nrm(ks[18], (N_EXPERTS,)),
        "w_gate": nrm(ks[19], (DEPTH, N_EXPERTS, D_MODEL, D_EXPERT)) * (D_MODEL ** -0.5),
        "w_up": nrm(ks[20], (DEPTH, N_EXPERTS, D_MODEL, D_EXPERT)) * (D_MODEL ** -0.5) * DEEPNORM_BETA,
        "w_down": nrm(ks[21], (DEPTH, N_EXPERTS, D_EXPERT, D_MODEL)) * (D_EXPERT ** -0.5) * DEEPNORM_BETA,
    }


def reference(x_prompt, x_sample, cache_fox_k, cache_fox_v, cache_fox_logf, cache_dsa_k,
              cache_dsa_v, cache_idx_k, page_table, meta_tokens, w_in, b_forget, w_out,
              ln1_g, ln1_b, ln2_g, ln2_b, w_router, b_router, w_gate, w_up, w_down):
    b = x_prompt.shape[0]
    meta = jnp.broadcast_to(meta_tokens[None].astype(x_prompt.dtype), (b, N_META, x_prompt.shape[-1]))
    xp = jnp.concatenate([meta, x_prompt], axis=1)
    xs = x_sample
    rows_p = [[] for _ in range(6)]
    rows_s = [[] for _ in range(6)]
    for l in range(DEPTH):
        mp, new_p = _prompt_mixer(xp, w_in[l], b_forget[l], w_out[l])
        ms, new_s = _sample_mixer(xs, cache_fox_k, cache_fox_v, cache_fox_logf, cache_dsa_k,
                                  cache_dsa_v, cache_idx_k, l, page_table, w_in[l],
                                  b_forget[l], w_out[l])
        for i in range(6):
            rows_p[i].append(new_p[i])
            rows_s[i].append(new_s[i])
        xp = _layer_norm(DEEPNORM_ALPHA * xp + mp, ln1_g[l], ln1_b[l])
        xs = _layer_norm(DEEPNORM_ALPHA * xs + ms, ln1_g[l], ln1_b[l])
        xp = _layer_norm(DEEPNORM_ALPHA * xp + _moe(xp, w_router, b_router, w_gate[l], w_up[l], w_down[l]),
                         ln2_g[l], ln2_b[l])
        xs = _layer_norm(DEEPNORM_ALPHA * xs + _moe(xs, w_router, b_router, w_gate[l], w_up[l], w_down[l]),
                         ln2_g[l], ln2_b[l])
    y_prompt = xp[:, N_META:]
    y_sample = xs
    return (y_prompt, y_sample,
            jnp.stack(rows_p[0]), jnp.stack(rows_p[1]), jnp.stack(rows_p[2]),
            jnp.stack(rows_p[3]), jnp.stack(rows_p[4]), jnp.stack(rows_p[5]),
            jnp.stack(rows_s[0]), jnp.stack(rows_s[1]), jnp.stack(rows_s[2]),
            jnp.stack(rows_s[3]), jnp.stack(rows_s[4]), jnp.stack(rows_s[5]))
```

```python
import functools

import jax
import jax.numpy as jnp
from jax import lax
from jax.experimental import pallas as pl
from jax.experimental.pallas import tpu as pltpu

F32 = jnp.float32
BF16 = jnp.bfloat16
I32 = jnp.int32

HEAD_DIM = 64
FOX_HEADS = 8
DSA_HEADS = 8
DSA_KV_HEADS = 2
DSA_GROUP = DSA_HEADS // DSA_KV_HEADS
IDX_HEADS = 4
IDX_DIM = 64
MAX_SELECT = 256
N_META = 16
PAGE_SIZE = 128
N_EXPERTS = 16
N_EXPERT_GROUPS = 4
EXPERTS_PER_GROUP = N_EXPERTS // N_EXPERT_GROUPS
D_EXPERT = 256
LN_EPS = 1e-5

LANE = 128
FOX_WIDTH = FOX_HEADS * HEAD_DIM
DSA_WIDTH = DSA_HEADS * HEAD_DIM
DSA_KV_WIDTH = DSA_KV_HEADS * HEAD_DIM
C_FQ = 0
C_FK = C_FQ + FOX_WIDTH
C_FV = C_FK + FOX_WIDTH
C_DQ = C_FV + FOX_WIDTH
C_DKV = C_DQ + DSA_WIDTH
C_IQ = C_DKV + 2 * DSA_KV_WIDTH
C_MISC = C_IQ + IDX_HEADS * LANE
N_PACKED = C_MISC + LANE
M_LOGF = IDX_DIM
M_IW = M_LOGF + FOX_HEADS

NEG = -1e30
INT_MIN = -(2 ** 31)
VMEM_LIMIT = 56 * 1024 * 1024


def _nt_dot(a, b):
    return lax.dot_general(a, b, (((1,), (1,)), ((), ())), preferred_element_type=F32)


def _dot(a, b):
    return jnp.dot(a, b, preferred_element_type=F32)


def _split3(x):
    hi = x.astype(BF16)
    r1 = x - hi.astype(F32)
    mid = r1.astype(BF16)
    lo = (r1 - mid.astype(F32)).astype(BF16)
    return hi, mid, lo


def _upper_tri(n):
    r = lax.broadcasted_iota(I32, (n, n), 0)
    c = lax.broadcasted_iota(I32, (n, n), 1)
    return jnp.where(r <= c, 1.0, 0.0).astype(BF16)


def _layer_norm(z, g, b):
    mu = jnp.mean(z, axis=-1, keepdims=True)
    zc = z - mu
    var = jnp.mean(zc * zc, axis=-1, keepdims=True)
    return zc * lax.rsqrt(var + LN_EPS) * g + b


def _sort_key(x):
    x = jnp.where(x == 0.0, 0.0, x)
    b = lax.bitcast_convert_type(x, I32)
    return b ^ ((b >> 31) & 0x7FFFFFFF)


def _softmax_pv(s, v_list):
    m = jnp.max(s, axis=1, keepdims=True)
    p = jnp.exp(s - m)
    l = jnp.sum(p, axis=1, keepdims=True)
    pb = p.astype(BF16)
    o = None
    off = 0
    for v in v_list:
        w = v.shape[0]
        t = _dot(pb[:, off:off + w], v)
        o = t if o is None else o + t
        off += w
    return o / l


def _proj_kernel(x_ref, w_ref, bf_ref, ofk, ofv, odkv, omisc, afq, afk, afv, adq, adkv, aiq, aik):
    x = x_ref[...].astype(BF16)

    def seg(a, b):
        return _dot(x, w_ref[:, a:b])

    afq[...] = (seg(C_FQ, C_FK) * (HEAD_DIM ** -0.5)).astype(BF16)
    fk = seg(C_FK, C_FV)
    ofk[...] = fk
    afk[...] = fk.astype(BF16)
    fv = seg(C_FV, C_DQ)
    ofv[...] = fv
    afv[...] = fv.astype(BF16)
    adq[...] = (seg(C_DQ, C_DKV) * (HEAD_DIM ** -0.5)).astype(BF16)
    dkv = seg(C_DKV, C_IQ)
    odkv[...] = dkv
    adkv[...] = dkv.astype(BF16)
    aiq[...] = seg(C_IQ, C_MISC).astype(BF16)
    m = seg(C_MISC, N_PACKED)
    lane = lax.broadcasted_iota(I32, m.shape, 1)
    z = m + bf_ref[...]
    logf = jnp.minimum(z, 0.0) - jnp.log1p(jnp.exp(-jnp.abs(z)))
    iw = m * (IDX_HEADS ** -0.5 * IDX_DIM ** -0.5)
    is_f = (lane >= M_LOGF) & (lane < M_IW)
    is_w = (lane >= M_IW) & (lane < M_IW + IDX_HEADS)
    omisc[...] = jnp.where(is_f, logf, jnp.where(is_w, iw, m))
    aik[...] = m.astype(BF16)


def _proj(x, w_packed, bf_packed, layer, tm):
    rt, d = x.shape
    row = lambda c: pl.BlockSpec((tm, c), lambda i: (i, 0))
    outs = [(FOX_WIDTH, F32), (FOX_WIDTH, F32), (2 * DSA_KV_WIDTH, F32), (LANE, F32),
            (FOX_WIDTH, BF16), (FOX_WIDTH, BF16), (FOX_WIDTH, BF16), (DSA_WIDTH, BF16),
            (2 * DSA_KV_WIDTH, BF16), (IDX_HEADS * LANE, BF16), (LANE, BF16)]
    return pl.pallas_call(
        _proj_kernel,
        grid=(rt // tm,),
        in_specs=[row(d),
                  pl.BlockSpec((None, d, N_PACKED), lambda i: (layer, 0, 0)),
                  pl.BlockSpec((None, 1, LANE), lambda i: (layer, 0, 0))],
        out_specs=[row(c) for c, _ in outs],
        out_shape=[jax.ShapeDtypeStruct((rt, c), dt) for c, dt in outs],
        compiler_params=pltpu.CompilerParams(dimension_semantics=("arbitrary",),
                                             vmem_limit_bytes=VMEM_LIMIT),
        name="proj",
    )(x, w_packed, bf_packed)


def _cumsum_lanes(x, u, carry):
    hi, mid, lo = _split3(x)
    return _dot(hi, u) + _dot(mid, u) + _dot(lo, u) + carry


def _cumsum_kernel(lf_ref, c_ref):
    n = lf_ref.shape[-1] // LANE
    u = _upper_tri(LANE)
    carry = jnp.zeros((lf_ref.shape[1], 1), F32)
    for c in range(n):
        out = _cumsum_lanes(lf_ref[0, :, c * LANE:(c + 1) * LANE], u, carry)
        c_ref[0, :, c * LANE:(c + 1) * LANE] = out
        carry = out[:, LANE - 1:LANE]


def _cumsum(lf_t):
    b, h, lp = lf_t.shape
    spec = pl.BlockSpec((1, h, lp), lambda i: (i, 0, 0))
    return pl.pallas_call(
        _cumsum_kernel, grid=(b,), in_specs=[spec], out_specs=spec,
        out_shape=jax.ShapeDtypeStruct(lf_t.shape, F32),
        compiler_params=pltpu.CompilerParams(dimension_semantics=("arbitrary",)),
        name="cumsum",
    )(lf_t)


def _count_rows(pred_chunks, ones):
    acc = None
    for p in pred_chunks:
        t = jnp.where(p, 1.0, 0.0)
        acc = t if acc is None else acc + t
    return _dot(acc.astype(BF16), ones)


def _selection_bias(keys, tau, need, u_ones, causal_last):
    out = []
    carry = jnp.zeros(tau.shape, F32)
    for c, k in enumerate(keys):
        eq = k == tau
        eqf = jnp.where(eq, 1.0, 0.0)
        pr = _dot(eqf.astype(BF16), u_ones)
        rank = carry + pr[:, :LANE] - eqf
        b = jnp.where(k > tau, 0.0, jnp.where(eq, jnp.where(rank < need, 0.0, NEG), NEG))
        if c == len(keys) - 1 and causal_last is not None:
            b = jnp.where(causal_last, b, NEG)
        out.append(b)
        carry = carry + pr[:, LANE:]
    return out


def _prefix_consts():
    u = _upper_tri(LANE)
    ones = jnp.ones((LANE, LANE), BF16)
    return jnp.concatenate([u, ones], axis=1), ones


def _select_p_kernel(n_sel, iq_ref, ik_ref, misc_ref, out_ref, key_ref, tau_ref):
    lp = iq_ref.shape[0]
    nq = lp // LANE
    offs = [i * (i + 1) // 2 for i in range(nq)]
    u_ones, ones = _prefix_consts()
    rloc = lax.broadcasted_iota(I32, (LANE, LANE), 0)
    cloc = lax.broadcasted_iota(I32, (LANE, LANE), 1)
    diag = cloc <= rloc

    for i in range(nq):
        w = (i + 1) * LANE
        rows = slice(i * LANE, (i + 1) * LANE)
        q = jnp.concatenate([iq_ref[rows, h * LANE:h * LANE + IDX_DIM] for h in range(IDX_HEADS)], axis=0)
        d = _nt_dot(q, ik_ref[0:w, 0:IDX_DIM])
        sc = jnp.zeros((LANE, w), F32)
        for h in range(IDX_HEADS):
            wgt = misc_ref[rows, M_IW + h:M_IW + h + 1]
            sc = sc + jnp.maximum(d[h * LANE:(h + 1) * LANE], 0.0) * wgt
        for c in range(i + 1):
            s_c = sc[:, c * LANE:(c + 1) * LANE]
            if c == i:
                s_c = jnp.where(diag, s_c, -jnp.inf)
            key_ref[offs[i] + c] = _sort_key(s_c)
        tau_ref[i] = jnp.full((LANE, LANE), INT_MIN, I32)

    def bit_step(it, carry):
        bit = jnp.left_shift(jnp.int32(1), 31 - it)
        for i in range(nq):
            cand = tau_ref[i] + bit
            cnt = _count_rows([key_ref[offs[i] + c] >= cand for c in range(i + 1)], ones)
            tau_ref[i] = jnp.where(cnt >= n_sel, cand, tau_ref[i])
        return carry

    lax.fori_loop(0, 32, bit_step, 0)

    neg_blk = jnp.full((LANE, LANE), NEG, BF16)
    for i in range(nq):
        tau = tau_ref[i]
        keys = [key_ref[offs[i] + c] for c in range(i + 1)]
        need = n_sel - _count_rows([k > tau for k in keys], ones)
        bias = _selection_bias(keys, tau, need, u_ones, diag)
        for c in range(nq):
            blk = bias[c].astype(BF16) if c <= i else neg_blk
            out_ref[0, i * LANE:(i + 1) * LANE, c * LANE:(c + 1) * LANE] = blk


def _select_p(aiq, aik, omisc, b, lp, n_sel):
    nq = lp // LANE
    return pl.pallas_call(
        functools.partial(_select_p_kernel, n_sel),
        grid=(b,),
        in_specs=[pl.BlockSpec((lp, IDX_HEADS * LANE), lambda i: (i, 0)),
                  pl.BlockSpec((lp, LANE), lambda i: (i, 0)),
                  pl.BlockSpec((lp, LANE), lambda i: (i, 0))],
        out_specs=pl.BlockSpec((1, lp, lp), lambda i: (i, 0, 0)),
        out_shape=jax.ShapeDtypeStruct((b, lp, lp), BF16),
        scratch_shapes=[pltpu.VMEM((nq * (nq + 1) // 2, LANE, LANE), I32),
                        pltpu.VMEM((nq, LANE, LANE), I32)],
        compiler_params=pltpu.CompilerParams(dimension_semantics=("arbitrary",),
                                             vmem_limit_bytes=VMEM_LIMIT),
        name="select_p",
    )(aiq, aik, omisc)


def _alibi_slope(head):
    return 2.0 ** (-(8.0 / DSA_HEADS) * (head + 1))


def _attn_p_body(w, i, fq_ref, fk_ref, fv_ref, dq_ref, dkv_ref, cq_ref, ck_ref, mb_ref, of_ref, od_ref):
    tq = fq_ref.shape[0]
    lane = lax.broadcasted_iota(I32, (tq, LANE), 1)
    lo_half = lane < HEAD_DIM
    qpos = i * tq + lax.broadcasted_iota(I32, (tq, w), 0)
    kpos = lax.broadcasted_iota(I32, (tq, w), 1)
    causal = kpos <= qpos

    for j in range(FOX_HEADS // 2):
        cols = slice(j * LANE, (j + 1) * LANE)
        qp = fq_ref[:, cols]
        q2 = jnp.concatenate([jnp.where(lo_half, qp, 0), jnp.where(lo_half, 0, qp)], axis=0)
        s = _nt_dot(q2, fk_ref[0:w, cols])
        bias = []
        for h in (2 * j, 2 * j + 1):
            bias.append(cq_ref[0, :, h:h + 1] - ck_ref[0, h:h + 1, 0:w])
        s = s + jnp.concatenate(bias, axis=0)
        s = jnp.where(jnp.concatenate([causal, causal], axis=0), s, NEG)
        o = _softmax_pv(s, [fv_ref[0:w, cols]])
        of_ref[:, cols] = jnp.where(lo_half, o[:tq], o[tq:]).astype(BF16)

    dist = (qpos - kpos).astype(F32)
    mb = mb_ref[0, :, 0:w].astype(F32)
    outs = []
    for g in range(DSA_KV_HEADS):
        keep = lo_half if g == 0 else jnp.logical_not(lo_half)
        q4 = jnp.concatenate(
            [jnp.where(keep, dq_ref[:, r * LANE:(r + 1) * LANE], 0) for r in range(DSA_GROUP)], axis=0)
        s = _nt_dot(q4, dkv_ref[0:w, 0:LANE])
        bias = [mb - _alibi_slope(g * DSA_GROUP + r) * dist for r in range(DSA_GROUP)]
        s = s + jnp.concatenate(bias, axis=0)
        outs.append(_softmax_pv(s, [dkv_ref[0:w, LANE:2 * LANE]]))
    for r in range(DSA_GROUP):
        rows = slice(r * tq, (r + 1) * tq)
        od_ref[:, r * LANE:(r + 1) * LANE] = jnp.where(lo_half, outs[0][rows], outs[1][rows]).astype(BF16)


def _attn_p_kernel(buckets, *refs):
    i = pl.program_id(1)
    for lo, hi, w in buckets:
        @pl.when((i >= lo) & (i < hi))
        def _():
            _attn_p_body(w, i, *refs)


def _attn_buckets(nq):
    step = 4
    return tuple((lo, min(lo + step, nq), min(lo + step, nq) * LANE) for lo in range(0, nq, step))


def _attn_p(afq, afk, afv, adq, adkv, c_col, c_t, mb, b, lp):
    nq = lp // LANE
    qblk = lambda c: pl.BlockSpec((LANE, c), lambda bi, i: (bi * nq + i, 0))
    seq = lambda c: pl.BlockSpec((lp, c), lambda bi, i: (bi, 0))
    out = jax.ShapeDtypeStruct((b * lp, FOX_WIDTH), BF16)
    return pl.pallas_call(
        functools.partial(_attn_p_kernel, _attn_buckets(nq)),
        grid=(b, nq),
        in_specs=[qblk(FOX_WIDTH), seq(FOX_WIDTH), seq(FOX_WIDTH), qblk(DSA_WIDTH), seq(2 * DSA_KV_WIDTH),
                  pl.BlockSpec((1, LANE, FOX_HEADS), lambda bi, i: (bi, i, 0)),
                  pl.BlockSpec((1, FOX_HEADS, lp), lambda bi, i: (bi, 0, 0)),
                  pl.BlockSpec((1, LANE, lp), lambda bi, i: (bi, i, 0))],
        out_specs=[qblk(FOX_WIDTH), qblk(DSA_WIDTH)],
        out_shape=[out, out],
        compiler_params=pltpu.CompilerParams(dimension_semantics=("arbitrary", "arbitrary"),
                                             vmem_limit_bytes=VMEM_LIMIT),
        name="attn_p",
    )(afq, afk, afv, adq, adkv, c_col, c_t, mb)


def _select_s_kernel(layer, n_sel, group, pt_ref, cache_ref, iq_ref, misc_ref, out_ref,
                     ik_buf, key_ref, sem):
    n_pages = pt_ref.shape[1]
    past = n_pages * PAGE_SIZE
    t_new = iq_ref.shape[1]
    pw = past + LANE
    step = pl.program_id(0)

    def page_copy(g, p):
        page = pt_ref[step * group + g, p]
        return pltpu.make_async_copy(cache_ref.at[layer, page],
                                     ik_buf.at[g, pl.ds(p * PAGE_SIZE, PAGE_SIZE)], sem)

    for g in range(group):
        for p in range(n_pages):
            page_copy(g, p).start()
    for g in range(group):
        for p in range(n_pages):
            page_copy(g, p).wait()

    col = lax.broadcasted_iota(I32, (t_new, pw), 1)
    tok = lax.broadcasted_iota(I32, (t_new, pw), 0)
    visible = (col < past) | (col - past <= tok)
    for g in range(group):
        qf = iq_ref[g]
        q = jnp.concatenate([qf[:, h * LANE:h * LANE + IDX_DIM] for h in range(IDX_HEADS)], axis=0).astype(BF16)
        k_new = jnp.concatenate([misc_ref[g][:, 0:IDX_DIM],
                                 jnp.zeros((LANE - t_new, IDX_DIM), F32)], axis=0).astype(BF16)
        d = jnp.concatenate([_nt_dot(q, ik_buf[g].astype(BF16)), _nt_dot(q, k_new)], axis=1)
        sc = jnp.zeros((t_new, pw), F32)
        for h in range(IDX_HEADS):
            wgt = misc_ref[g][:, M_IW + h:M_IW + h + 1]
            sc = sc + jnp.maximum(d[h * t_new:(h + 1) * t_new], 0.0) * wgt
        key_ref[g * t_new:(g + 1) * t_new, :] = _sort_key(jnp.where(visible, sc, -jnp.inf))

    rows = group * t_new
    nch = pw // LANE
    u_ones, ones = _prefix_consts()
    keys = [key_ref[:, c * LANE:(c + 1) * LANE] for c in range(nch)]

    def bit_step(it, tau):
        cand = tau + jnp.left_shift(jnp.int32(1), 31 - it)
        cnt = _count_rows([k >= cand for k in keys], ones)
        return jnp.where(cnt >= n_sel, cand, tau)

    tau = lax.fori_loop(0, 32, bit_step, jnp.full((rows, LANE), INT_MIN, I32))
    need = n_sel - _count_rows([k > tau for k in keys], ones)
    bias = _selection_bias(keys, tau, need, u_ones, None)
    bias = jnp.where(jnp.concatenate([visible] * group, axis=0), jnp.concatenate(bias, axis=1), NEG)
    for g in range(group):
        out_ref[g] = bias[g * t_new:(g + 1) * t_new]


def _select_s(page_table, cache_ik, iq_s, misc_s, layer, n_sel, group):
    db, t_new, _ = iq_s.shape
    n_pages = page_table.shape[1]
    past = n_pages * PAGE_SIZE
    pw = past + LANE
    blk = lambda c: pl.BlockSpec((group, t_new, c), lambda i, pt: (i, 0, 0))
    return pl.pallas_call(
        functools.partial(_select_s_kernel, layer, n_sel, group),
        grid_spec=pltpu.PrefetchScalarGridSpec(
            num_scalar_prefetch=1, grid=(db // group,),
            in_specs=[pl.BlockSpec(memory_space=pl.ANY), blk(IDX_HEADS * LANE), blk(LANE)],
            out_specs=blk(pw),
            scratch_shapes=[pltpu.VMEM((group, past, IDX_DIM), F32),
                            pltpu.VMEM((group * t_new, pw), I32),
                            pltpu.SemaphoreType.DMA(())]),
        out_shape=jax.ShapeDtypeStruct((db, t_new, pw), F32),
        compiler_params=pltpu.CompilerParams(dimension_semantics=("arbitrary",),
                                             vmem_limit_bytes=VMEM_LIMIT),
        name="select_s",
    )(page_table, cache_ik, iq_s, misc_s)


def _pad_rows(x, n):
    return jnp.concatenate([x, jnp.zeros((n - x.shape[0], x.shape[1]), x.dtype)], axis=0)


def _attn_s_kernel(layer, pt_ref, cfk_ref, cfv_ref, clf_ref, cdk_ref, cdv_ref,
                   fq_ref, dq_ref, fk_ref, fv_ref, dkv_ref, lf_ref, mb_ref, of_ref, od_ref,
                   kbuf, vbuf, lfbuf, dkbuf, dvbuf, sems):
    n_pages = pt_ref.shape[1]
    past = n_pages * PAGE_SIZE
    t_new = fq_ref.shape[1]
    pw = past + LANE
    step = pl.program_id(0)
    nsteps = pl.num_programs(0)
    slot = step % 2

    def copies(seq, sl):
        out = []
        for p in range(n_pages):
            page = pt_ref[seq, p]
            rows = pl.ds(p * PAGE_SIZE, PAGE_SIZE)
            out.append(pltpu.make_async_copy(cfk_ref.at[layer, page], kbuf.at[sl, rows], sems.at[sl, 0]))
            out.append(pltpu.make_async_copy(cfv_ref.at[layer, page], vbuf.at[sl, rows], sems.at[sl, 1]))
            out.append(pltpu.make_async_copy(clf_ref.at[layer, page], lfbuf.at[sl, :, rows], sems.at[sl, 2]))
            out.append(pltpu.make_async_copy(cdk_ref.at[layer, page], dkbuf.at[sl, rows], sems.at[sl, 3]))
            out.append(pltpu.make_async_copy(cdv_ref.at[layer, page], dvbuf.at[sl, rows], sems.at[sl, 4]))
        return out

    @pl.when(step == 0)
    def _():
        for c in copies(step, slot):
            c.start()

    @pl.when(step + 1 < nsteps)
    def _():
        for c in copies(step + 1, 1 - slot):
            c.start()

    for c in copies(step, slot):
        c.wait()

    rows = t_new * FOX_HEADS
    r_idx = lax.broadcasted_iota(I32, (rows, pw), 0)
    col = lax.broadcasted_iota(I32, (rows, pw), 1)

    tok = r_idx // FOX_HEADS
    visible = (col < past) | (col - past <= tok)
    lane_w = lax.broadcasted_iota(I32, (rows, FOX_WIDTH), 1)
    row_w = lax.broadcasted_iota(I32, (rows, FOX_WIDTH), 0)
    head_mask = (lane_w // HEAD_DIM) == (row_w % FOX_HEADS)
    q = fq_ref[0]
    q_bd = jnp.where(head_mask, jnp.broadcast_to(q[:, None, :], (t_new, FOX_HEADS, FOX_WIDTH)).reshape(rows, FOX_WIDTH), 0.0)
    q_bd = q_bd.astype(BF16)
    k_past = kbuf[slot].astype(BF16)
    v_past = vbuf[slot].astype(BF16)
    k_new = _pad_rows(fk_ref[0], LANE).astype(BF16)
    v_new = _pad_rows(fv_ref[0], LANE).astype(BF16)
    s = jnp.concatenate([_nt_dot(q_bd, k_past), _nt_dot(q_bd, k_new)], axis=1)

    u = _upper_tri(LANE)
    carry = jnp.zeros((FOX_HEADS, 1), F32)
    cks = []
    for p in range(n_pages):
        out = _cumsum_lanes(lfbuf[slot, :, p * PAGE_SIZE:(p + 1) * PAGE_SIZE], u, carry)
        cks.append(out)
        carry = out[:, LANE - 1:LANE]
    c_new = _cumsum_lanes(lf_ref[0], u, carry)
    cks.append(c_new)
    ck = jnp.concatenate(cks, axis=1)
    ck_rows = jnp.concatenate([ck] * t_new, axis=0)
    c_new_rows = jnp.concatenate([c_new] * t_new, axis=0)
    pick = lax.broadcasted_iota(I32, (rows, LANE), 1) == lax.broadcasted_iota(I32, (rows, LANE), 0) // FOX_HEADS
    cq = jnp.sum(jnp.where(pick, c_new_rows, 0.0), axis=1, keepdims=True)
    s = jnp.where(visible, s + (cq - ck_rows), NEG)
    o = _softmax_pv(s, [v_past, v_new])
    o = jnp.where(head_mask, o, 0.0).reshape(t_new, FOX_HEADS, FOX_WIDTH)
    of_ref[0] = jnp.sum(o, axis=1)

    tok_d = r_idx % t_new
    piece = lax.broadcasted_iota(I32, (rows, 1), 0) // t_new
    head = (piece % 2) * DSA_GROUP + piece // 2
    slope = jnp.zeros((rows, 1), F32)
    for h in range(DSA_HEADS):
        slope = jnp.where(head == h, _alibi_slope(h), slope)
    dq = dq_ref[0]
    lane_b = lax.broadcasted_iota(I32, (t_new, LANE), 1)
    pieces = []
    for r in range(DSA_GROUP):
        blk = dq[:, r * LANE:(r + 1) * LANE]
        pieces.append(jnp.where(lane_b < HEAD_DIM, blk, 0.0))
        pieces.append(jnp.where(lane_b < HEAD_DIM, 0.0, blk))
    q_d = jnp.concatenate(pieces, axis=0).astype(BF16)
    dk_past = dkbuf[slot].astype(BF16)
    dv_past = dvbuf[slot].astype(BF16)
    dkv_new = _pad_rows(dkv_ref[0], LANE).astype(BF16)
    s = jnp.concatenate([_nt_dot(q_d, dk_past), _nt_dot(q_d, dkv_new[:, 0:LANE])], axis=1)
    mb = jnp.concatenate([mb_ref[0]] * (rows // t_new), axis=0)
    dist = (past + tok_d - col).astype(F32)
    s = s + (mb - slope * dist)
    o = _softmax_pv(s, [dv_past, dkv_new[:, LANE:2 * LANE]])
    lane_o = lax.broadcasted_iota(I32, (t_new, LANE), 1)
    blocks = []
    for r in range(DSA_GROUP):
        a = o[(2 * r) * t_new:(2 * r + 1) * t_new]
        b = o[(2 * r + 1) * t_new:(2 * r + 2) * t_new]
        blocks.append(jnp.where(lane_o < HEAD_DIM, a, b))
    od_ref[0] = jnp.concatenate(blocks, axis=1)


def _attn_s(page_table, cfk, cfv, clf_t, cdk, cdv, fq_s, dq_s, fk_s, fv_s, dkv_s, lf_s, mb_s, layer):
    db, t_new, _ = fq_s.shape
    n_pages = page_table.shape[1]
    past = n_pages * PAGE_SIZE
    pw = past + LANE
    blk = lambda r, c: pl.BlockSpec((1, r, c), lambda i, pt: (i, 0, 0))
    anyspec = pl.BlockSpec(memory_space=pl.ANY)
    out = jax.ShapeDtypeStruct((db, t_new, FOX_WIDTH), F32)
    return pl.pallas_call(
        functools.partial(_attn_s_kernel, layer),
        grid_spec=pltpu.PrefetchScalarGridSpec(
            num_scalar_prefetch=1, grid=(db,),
            in_specs=[anyspec] * 5 + [blk(t_new, FOX_WIDTH), blk(t_new, DSA_WIDTH), blk(t_new, FOX_WIDTH),
                                      blk(t_new, FOX_WIDTH), blk(t_new, 2 * DSA_KV_WIDTH),
                                      blk(FOX_HEADS, LANE), blk(t_new, pw)],
            out_specs=[blk(t_new, FOX_WIDTH), blk(t_new, DSA_WIDTH)],
            scratch_shapes=[pltpu.VMEM((2, past, FOX_WIDTH), F32),
                            pltpu.VMEM((2, past, FOX_WIDTH), F32),
                            pltpu.VMEM((2, FOX_HEADS, past), F32),
                            pltpu.VMEM((2, past, DSA_KV_WIDTH), F32),
                            pltpu.VMEM((2, past, DSA_KV_WIDTH), F32),
                            pltpu.SemaphoreType.DMA((2, 5))]),
        out_shape=[out, out],
        compiler_params=pltpu.CompilerParams(dimension_semantics=("arbitrary",),
                                             vmem_limit_bytes=VMEM_LIMIT),
        name="attn_s",
    )(page_table, cfk, cfv, clf_t, cdk, cdv, fq_s, dq_s, fk_s, fv_s, dkv_s, lf_s, mb_s)


def _router_gates(s, sb):
    row = lambda a, e: a[e:e + 1, :]
    gscore = []
    for g in range(N_EXPERT_GROUPS):
        a, b, c, d = (row(sb, EXPERTS_PER_GROUP * g + k) for k in range(EXPERTS_PER_GROUP))
        m1, n1 = jnp.maximum(a, b), jnp.minimum(a, b)
        m2, n2 = jnp.maximum(c, d), jnp.minimum(c, d)
        gscore.append(jnp.maximum(m1, m2) + jnp.maximum(jnp.minimum(m1, m2), jnp.maximum(n1, n2)))
    best, gi = gscore[0], jnp.zeros(gscore[0].shape, I32)
    for g in range(1, N_EXPERT_GROUPS):
        better = gscore[g] > best
        gi = jnp.where(better, g, gi)
        best = jnp.where(better, gscore[g], best)

    def in_group(arr, k):
        out = row(arr, k)
        for g in range(1, N_EXPERT_GROUPS):
            out = jnp.where(gi == g, row(arr, EXPERTS_PER_GROUP * g + k), out)
        return out

    vb = [in_group(sb, k) for k in range(EXPERTS_PER_GROUP)]
    vs = [in_group(s, k) for k in range(EXPERTS_PER_GROUP)]

    def arg_first_max(vals):
        best, idx = vals[0], jnp.zeros(vals[0].shape, I32)
        for k in range(1, len(vals)):
            better = vals[k] > best
            idx = jnp.where(better, k, idx)
            best = jnp.where(better, vals[k], best)
        return idx

    i1 = arg_first_max(vb)
    i2 = arg_first_max([jnp.where(i1 == k, -jnp.inf, vb[k]) for k in range(EXPERTS_PER_GROUP)])

    def take(vals, idx):
        out = vals[0]
        for k in range(1, len(vals)):
            out = jnp.where(idx == k, vals[k], out)
        return out

    g1, g2 = take(vs, i1), take(vs, i2)
    den = g1 + g2
    g1, g2 = g1 / den, g2 / den
    e1 = gi * EXPERTS_PER_GROUP + i1
    e2 = gi * EXPERTS_PER_GROUP + i2
    rows = [jnp.where(e1 == e, g1, jnp.where(e2 == e, g2, 0.0)) for e in range(N_EXPERTS)]
    return jnp.concatenate(rows, axis=0)


def _outproj_kernel(alpha, x_ref, mf_ref, md_ref, wf_ref, wd_ref, g_ref, b_ref, wr_ref, br_ref,
                    x1_ref, gates_ref):
    y = _dot(mf_ref[...], wf_ref[...]) + _dot(md_ref[...], wd_ref[...])
    x1 = _layer_norm(alpha * x_ref[...] + y, g_ref[...], b_ref[...])
    x1_ref[...] = x1
    xh, xm, _ = _split3(x1)
    wh, wm, _ = _split3(wr_ref[...])
    logits = _nt_dot(wh, xh) + (_nt_dot(wh, xm) + _nt_dot(wm, xh))
    s = 1.0 / (1.0 + jnp.exp(-logits))
    gates_ref[...] = _router_gates(s, s + br_ref[...])


def _outproj(x, mix_f, mix_d, wo_f, wo_d, g, b, wr_t, br, layer, alpha, tm):
    rt, d = x.shape
    row = lambda c: pl.BlockSpec((tm, c), lambda i: (i, 0))
    lay = lambda r, c: pl.BlockSpec((None, r, c), lambda i: (layer, 0, 0))
    full = lambda r, c: pl.BlockSpec((r, c), lambda i: (0, 0))
    return pl.pallas_call(
        functools.partial(_outproj_kernel, alpha),
        grid=(rt // tm,),
        in_specs=[row(d), row(FOX_WIDTH), row(DSA_WIDTH), lay(FOX_WIDTH, d), lay(DSA_WIDTH, d),
                  lay(1, d), lay(1, d), full(N_EXPERTS, d), full(N_EXPERTS, 1)],
        out_specs=[row(d), pl.BlockSpec((N_EXPERTS, tm), lambda i: (0, i))],
        out_shape=[jax.ShapeDtypeStruct((rt, d), F32), jax.ShapeDtypeStruct((N_EXPERTS, rt), F32)],
        compiler_params=pltpu.CompilerParams(dimension_semantics=("arbitrary",),
                                             vmem_limit_bytes=VMEM_LIMIT),
        name="outproj",
    )(x, mix_f, mix_d, wo_f, wo_d, g, b, wr_t, br)


def _moe_kernel(alpha, x_ref, gate_ref, wg_ref, wu_ref, wd_ref, g_ref, b_ref, out_ref, acc_ref):
    j = pl.program_id(1)

    @pl.when(j == 0)
    def _():
        acc_ref[...] = jnp.zeros(acc_ref.shape, F32)

    xb = x_ref[...].astype(BF16)
    acc = acc_ref[...]
    for e in range(EXPERTS_PER_GROUP):
        hg = _dot(xb, wg_ref[e])
        hu = _dot(xb, wu_ref[e])
        a = (hg * (1.0 / (1.0 + jnp.exp(-hg)))) * hu * gate_ref[:, e:e + 1]
        acc = acc + _dot(a.astype(BF16), wd_ref[e])
    acc_ref[...] = acc

    @pl.when(j == pl.num_programs(1) - 1)
    def _():
        out_ref[...] = _layer_norm(alpha * x_ref[...] + acc_ref[...], g_ref[...], b_ref[...])


def _moe(x1, gates_g, wg, wu, wd, g, b, layer, alpha, tm):
    rt, d = x1.shape
    epg = EXPERTS_PER_GROUP
    return pl.pallas_call(
        functools.partial(_moe_kernel, alpha),
        grid=(rt // tm, N_EXPERT_GROUPS),
        in_specs=[pl.BlockSpec((tm, d), lambda i, j: (i, 0)),
                  pl.BlockSpec((None, tm, epg), lambda i, j: (j, i, 0)),
                  pl.BlockSpec((None, epg, d, D_EXPERT), lambda i, j: (layer, j, 0, 0)),
                  pl.BlockSpec((None, epg, d, D_EXPERT), lambda i, j: (layer, j, 0, 0)),
                  pl.BlockSpec((None, epg, D_EXPERT, d), lambda i, j: (layer, j, 0, 0)),
                  pl.BlockSpec((None, 1, d), lambda i, j: (layer, 0, 0)),
                  pl.BlockSpec((None, 1, d), lambda i, j: (layer, 0, 0))],
        out_specs=pl.BlockSpec((tm, d), lambda i, j: (i, 0)),
        out_shape=jax.ShapeDtypeStruct((rt, d), F32),
        scratch_shapes=[pltpu.VMEM((tm, d), F32)],
        compiler_params=pltpu.CompilerParams(dimension_semantics=("arbitrary", "arbitrary"),
                                             vmem_limit_bytes=VMEM_LIMIT),
        name="moe",
    )(x1, gates_g, wg, wu, wd, g, b)


def _pack_w_in(w_in, b_forget):
    depth, d, _ = w_in.shape
    o = 0
    seg = {}
    for name, size in (("fq", FOX_WIDTH), ("fk", FOX_WIDTH), ("fv", FOX_WIDTH), ("ff", FOX_HEADS),
                       ("dq", DSA_WIDTH), ("dk", DSA_KV_WIDTH), ("dv", DSA_KV_WIDTH),
                       ("iq", IDX_HEADS * IDX_DIM), ("ik", IDX_DIM), ("iw", IDX_HEADS)):
        seg[name] = w_in[:, :, o:o + size]
        o += size
    head_order = [h for r in range(DSA_GROUP) for h in (r, DSA_GROUP + r)]
    dq = seg["dq"].reshape(depth, d, DSA_HEADS, HEAD_DIM)[:, :, head_order].reshape(depth, d, DSA_WIDTH)
    iq = jnp.pad(seg["iq"].reshape(depth, d, IDX_HEADS, IDX_DIM), ((0, 0), (0, 0), (0, 0), (0, LANE - IDX_DIM)))
    iq = iq.reshape(depth, d, IDX_HEADS * LANE)
    misc = jnp.concatenate([seg["ik"], seg["ff"], seg["iw"],
                            jnp.zeros((depth, d, LANE - M_IW - IDX_HEADS), w_in.dtype)], axis=2)
    w = jnp.concatenate([seg["fq"], seg["fk"], seg["fv"], dq, seg["dk"], seg["dv"], iq, misc], axis=2)
    bf = jnp.pad(b_forget.astype(F32), ((0, 0), (M_LOGF, LANE - M_IW)))[:, None, :]
    return w.astype(BF16), bf, head_order


def _row_tile(rt, cap):
    t = cap
    while rt % t:
        t //= 2
    return t


def kernel(x_prompt, x_sample, cache_fox_k, cache_fox_v, cache_fox_logf, cache_dsa_k, cache_dsa_v,
           cache_idx_k, page_table, meta_tokens, w_in, b_forget, w_out, ln1_g, ln1_b, ln2_g, ln2_b,
           w_router, b_router, w_gate, w_up, w_down):
    b, seq, d = x_prompt.shape
    db, t_new, _ = x_sample.shape
    depth = w_in.shape[0]
    n_pool = cache_fox_k.shape[1]
    n_pages = page_table.shape[1]
    past = n_pages * PAGE_SIZE
    l = seq + N_META
    lp = -(-l // LANE) * LANE
    rp = b * lp
    rt = rp + db * t_new
    alpha = (2.0 * depth) ** 0.25
    n_sel_p = min(MAX_SELECT, (l - N_META) // 4)
    n_sel_s = min(MAX_SELECT, (past + t_new) // 4)
    group_s = min(16, db)

    w_packed, bf_packed, head_order = _pack_w_in(w_in, b_forget)
    wo_f = w_out[:, :FOX_WIDTH].astype(BF16)
    wo_d = w_out[:, FOX_WIDTH:].reshape(depth, DSA_HEADS, HEAD_DIM, d)[:, head_order].reshape(depth, DSA_WIDTH, d)
    wo_d = wo_d.astype(BF16)
    wr_t = w_router.T.astype(F32)
    br = b_router.astype(F32)[:, None]
    wg, wu, wd = w_gate.astype(BF16), w_up.astype(BF16), w_down.astype(BF16)
    ln = lambda a: a.astype(F32)[:, None, :]
    g1, b1, g2, b2 = ln(ln1_g), ln(ln1_b), ln(ln2_g), ln(ln2_b)
    cfk = cache_fox_k.reshape(depth, n_pool, PAGE_SIZE, FOX_WIDTH)
    cfv = cache_fox_v.reshape(depth, n_pool, PAGE_SIZE, FOX_WIDTH)
    clf_t = jnp.swapaxes(cache_fox_logf, 2, 3)
    cdk = cache_dsa_k.reshape(depth, n_pool, PAGE_SIZE, DSA_KV_WIDTH)
    cdv = cache_dsa_v.reshape(depth, n_pool, PAGE_SIZE, DSA_KV_WIDTH)

    meta = jnp.broadcast_to(meta_tokens[None].astype(F32), (b, N_META, d))
    xp = jnp.concatenate([meta, x_prompt, jnp.zeros((b, lp - l, d), F32)], axis=1)
    x = jnp.concatenate([xp.reshape(rp, d), x_sample.reshape(db * t_new, d)], axis=0)

    tm = _row_tile(rt, 512)
    tm_moe = _row_tile(rt, 1024)
    rows_p = [[] for _ in range(6)]
    rows_s = [[] for _ in range(6)]
    for layer in range(depth):
        ofk, ofv, odkv, omisc, afq, afk, afv, adq, adkv, aiq, aik = _proj(x, w_packed, bf_packed, layer, tm)

        lf_t = jnp.swapaxes(omisc[:rp, M_LOGF:M_IW].reshape(b, lp, FOX_HEADS), 1, 2)
        c_t = _cumsum(lf_t)
        c_col = jnp.swapaxes(c_t, 1, 2)
        mb_p = _select_p(aiq, aik, omisc, b, lp, n_sel_p)
        mf_p, md_p = _attn_p(afq, afk, afv, adq, adkv, c_col, c_t, mb_p, b, lp)

        smp = lambda a: a[rp:].reshape(db, t_new, a.shape[1]).astype(F32)
        misc_s = smp(omisc)
        mb_s = _select_s(page_table, cache_idx_k, smp(aiq), misc_s, layer, n_sel_s, group_s)
        lf_s = jnp.pad(jnp.swapaxes(misc_s[:, :, M_LOGF:M_IW], 1, 2), ((0, 0), (0, 0), (0, LANE - t_new)))
        mf_s, md_s = _attn_s(page_table, cfk, cfv, clf_t, cdk, cdv, smp(afq), smp(adq), smp(ofk), smp(ofv),
                             smp(odkv), lf_s, mb_s, layer)

        mix_f = jnp.concatenate([mf_p, mf_s.reshape(db * t_new, FOX_WIDTH).astype(BF16)], axis=0)
        mix_d = jnp.concatenate([md_p, md_s.reshape(db * t_new, DSA_WIDTH).astype(BF16)], axis=0)
        x1, gates_t = _outproj(x, mix_f, mix_d, wo_f, wo_d, g1, b1, wr_t, br, layer, alpha, tm)
        gates_g = jnp.swapaxes(gates_t.reshape(N_EXPERT_GROUPS, EXPERTS_PER_GROUP, rt), 1, 2)
        x = _moe(x1, gates_g, wg, wu, wd, g2, b2, layer, alpha, tm_moe)

        new = (ofk, ofv, omisc[:, M_LOGF:M_IW], odkv[:, :DSA_KV_WIDTH], odkv[:, DSA_KV_WIDTH:], omisc[:, :IDX_DIM])
        for k, a in enumerate(new):
            rows_p[k].append(a[:rp].reshape(b, lp, -1)[:, :l])
            rows_s[k].append(a[rp:].reshape(db, t_new, -1))

    y_prompt = x[:rp].reshape(b, lp, d)[:, N_META:l]
    y_sample = x[rp:].reshape(db, t_new, d)
    shapes = [(FOX_HEADS, HEAD_DIM), (FOX_HEADS, HEAD_DIM), (FOX_HEADS,), (DSA_KV_HEADS, HEAD_DIM),
              (DSA_KV_HEADS, HEAD_DIM), (IDX_DIM,)]
    outs_p = [jnp.stack(r).reshape((depth, b, l) + s) for r, s in zip(rows_p, shapes)]
    outs_s = [jnp.stack(r).reshape((depth, db, t_new) + s) for r, s in zip(rows_s, shapes)]
    return (y_prompt, y_sample, *outs_p, *outs_s)
```

```python
import functools

import jax
import jax.numpy as jnp
from jax import lax
from jax.experimental import pallas as pl
from jax.experimental.pallas import tpu as pltpu

F32 = jnp.float32
BF16 = jnp.bfloat16
I32 = jnp.int32

HEAD_DIM = 64
FOX_HEADS = 8
DSA_HEADS = 8
DSA_KV_HEADS = 2
DSA_GROUP = DSA_HEADS // DSA_KV_HEADS
IDX_HEADS = 4
IDX_DIM = 64
MAX_SELECT = 256
N_META = 16
PAGE_SIZE = 128
N_EXPERTS = 16
N_EXPERT_GROUPS = 4
EXPERTS_PER_GROUP = N_EXPERTS // N_EXPERT_GROUPS
D_EXPERT = 256
LN_EPS = 1e-5

LANE = 128
FOX_WIDTH = FOX_HEADS * HEAD_DIM
DSA_WIDTH = DSA_HEADS * HEAD_DIM
DSA_KV_WIDTH = DSA_KV_HEADS * HEAD_DIM
C_FQ = 0
C_FK = C_FQ + FOX_WIDTH
C_FV = C_FK + FOX_WIDTH
C_DQ = C_FV + FOX_WIDTH
C_DKV = C_DQ + DSA_WIDTH
C_IQ = C_DKV + 2 * DSA_KV_WIDTH
C_MISC = C_IQ + IDX_HEADS * LANE
N_PACKED = C_MISC + LANE
M_LOGF = IDX_DIM
M_IW = M_LOGF + FOX_HEADS

NEG = -1e30
INT_MIN = -(2 ** 31)
VMEM_LIMIT = 56 * 1024 * 1024


def _nt_dot(a, b):
    return lax.dot_general(a, b, (((1,), (1,)), ((), ())), preferred_element_type=F32)


def _dot(a, b):
    return jnp.dot(a, b, preferred_element_type=F32)


def _split3(x):
    hi = x.astype(BF16)
    r1 = x - hi.astype(F32)
    mid = r1.astype(BF16)
    lo = (r1 - mid.astype(F32)).astype(BF16)
    return hi, mid, lo


def _upper_tri(n):
    r = lax.broadcasted_iota(I32, (n, n), 0)
    c = lax.broadcasted_iota(I32, (n, n), 1)
    return jnp.where(r <= c, 1.0, 0.0).astype(BF16)


def _layer_norm(z, g, b):
    mu = jnp.mean(z, axis=-1, keepdims=True)
    zc = z - mu
    var = jnp.mean(zc * zc, axis=-1, keepdims=True)
    return zc * lax.rsqrt(var + LN_EPS) * g + b


def _sort_key(x):
    x = jnp.where(x == 0.0, 0.0, x)
    b = lax.bitcast_convert_type(x, I32)
    return b ^ ((b >> 31) & 0x7FFFFFFF)


def _softmax_pv(s, v_t, v_new):
    m = jnp.max(s, axis=1, keepdims=True)
    p = jnp.exp(s - m)
    l = jnp.sum(p, axis=1, keepdims=True)
    pb = p.astype(BF16)
    if v_t is None:
        return _dot(pb, v_new) / l
    w0 = v_t.shape[1]
    return (_nt_dot(pb[:, :w0], v_t) + _dot(pb[:, w0:], v_new)) / l


def _proj_kernel(x_ref, w_ref, bf_ref, ofk, ofv, odkv, omisc, afq, afk, afv, adq, adkv, aiq, aik):
    x = x_ref[...].astype(BF16)

    def seg(a, b):
        return _dot(x, w_ref[:, a:b])

    afq[...] = (seg(C_FQ, C_FK) * (HEAD_DIM ** -0.5)).astype(BF16)
    fk = seg(C_FK, C_FV)
    ofk[...] = fk
    afk[...] = fk.astype(BF16)
    fv = seg(C_FV, C_DQ)
    ofv[...] = fv
    afv[...] = fv.astype(BF16)
    adq[...] = (seg(C_DQ, C_DKV) * (HEAD_DIM ** -0.5)).astype(BF16)
    dkv = seg(C_DKV, C_IQ)
    odkv[...] = dkv
    adkv[...] = dkv.astype(BF16)
    aiq[...] = seg(C_IQ, C_MISC).astype(BF16)
    m = seg(C_MISC, N_PACKED)
    lane = lax.broadcasted_iota(I32, m.shape, 1)
    z = m + bf_ref[...]
    logf = jnp.minimum(z, 0.0) - jnp.log1p(jnp.exp(-jnp.abs(z)))
    iw = m * (IDX_HEADS ** -0.5 * IDX_DIM ** -0.5)
    is_f = (lane >= M_LOGF) & (lane < M_IW)
    is_w = (lane >= M_IW) & (lane < M_IW + IDX_HEADS)
    omisc[...] = jnp.where(is_f, logf, jnp.where(is_w, iw, m))
    aik[...] = m.astype(BF16)


def _proj(x, w_packed, bf_packed, layer, tm):
    rt, d = x.shape
    row = lambda c: pl.BlockSpec((tm, c), lambda i: (i, 0))
    outs = [(FOX_WIDTH, F32), (FOX_WIDTH, F32), (2 * DSA_KV_WIDTH, F32), (LANE, F32),
            (FOX_WIDTH, BF16), (FOX_WIDTH, BF16), (FOX_WIDTH, BF16), (DSA_WIDTH, BF16),
            (2 * DSA_KV_WIDTH, BF16), (IDX_HEADS * LANE, BF16), (LANE, BF16)]
    return pl.pallas_call(
        _proj_kernel,
        grid=(rt // tm,),
        in_specs=[row(d),
                  pl.BlockSpec((None, d, N_PACKED), lambda i: (layer, 0, 0)),
                  pl.BlockSpec((None, 1, LANE), lambda i: (layer, 0, 0))],
        out_specs=[row(c) for c, _ in outs],
        out_shape=[jax.ShapeDtypeStruct((rt, c), dt) for c, dt in outs],
        compiler_params=pltpu.CompilerParams(dimension_semantics=("arbitrary",),
                                             vmem_limit_bytes=VMEM_LIMIT),
        name="proj",
    )(x, w_packed, bf_packed)


def _cumsum_lanes(x, u, carry):
    hi, mid, lo = _split3(x)
    return _dot(hi, u) + _dot(mid, u) + _dot(lo, u) + carry


def _cumsum_kernel(lf_ref, c_ref):
    n = lf_ref.shape[-1] // LANE
    u = _upper_tri(LANE)
    carry = jnp.zeros((lf_ref.shape[1], 1), F32)
    for c in range(n):
        out = _cumsum_lanes(lf_ref[0, :, c * LANE:(c + 1) * LANE], u, carry)
        c_ref[0, :, c * LANE:(c + 1) * LANE] = out
        carry = out[:, LANE - 1:LANE]


def _cumsum(lf_t):
    b, h, lp = lf_t.shape
    spec = pl.BlockSpec((1, h, lp), lambda i: (i, 0, 0))
    return pl.pallas_call(
        _cumsum_kernel, grid=(b,), in_specs=[spec], out_specs=spec,
        out_shape=jax.ShapeDtypeStruct(lf_t.shape, F32),
        compiler_params=pltpu.CompilerParams(dimension_semantics=("arbitrary",)),
        name="cumsum",
    )(lf_t)


def _count_rows(pred_chunks, ones):
    acc = None
    for p in pred_chunks:
        t = jnp.where(p, 1.0, 0.0)
        acc = t if acc is None else acc + t
    return _dot(acc.astype(BF16), ones)


def _selection_bias(keys, tau, need, u_ones, causal_last):
    out = []
    carry = jnp.zeros(tau.shape, F32)
    for c, k in enumerate(keys):
        eq = k == tau
        eqf = jnp.where(eq, 1.0, 0.0)
        pr = _dot(eqf.astype(BF16), u_ones)
        rank = carry + pr[:, :LANE] - eqf
        b = jnp.where(k > tau, 0.0, jnp.where(eq, jnp.where(rank < need, 0.0, NEG), NEG))
        if c == len(keys) - 1 and causal_last is not None:
            b = jnp.where(causal_last, b, NEG)
        out.append(b)
        carry = carry + pr[:, LANE:]
    return out


def _prefix_consts():
    u = _upper_tri(LANE)
    ones = jnp.ones((LANE, LANE), BF16)
    return jnp.concatenate([u, ones], axis=1), ones


def _select_p_kernel(n_sel, iq_ref, ik_ref, misc_ref, out_ref, key_ref, tau_ref):
    lp = iq_ref.shape[0]
    nq = lp // LANE
    offs = [i * (i + 1) // 2 for i in range(nq)]
    u_ones, ones = _prefix_consts()
    rloc = lax.broadcasted_iota(I32, (LANE, LANE), 0)
    cloc = lax.broadcasted_iota(I32, (LANE, LANE), 1)
    diag = cloc <= rloc

    for i in range(nq):
        w = (i + 1) * LANE
        rows = slice(i * LANE, (i + 1) * LANE)
        q = jnp.concatenate([iq_ref[rows, h * LANE:h * LANE + IDX_DIM] for h in range(IDX_HEADS)], axis=0)
        d = _nt_dot(q, ik_ref[0:w, 0:IDX_DIM])
        sc = jnp.zeros((LANE, w), F32)
        for h in range(IDX_HEADS):
            wgt = misc_ref[rows, M_IW + h:M_IW + h + 1]
            sc = sc + jnp.maximum(d[h * LANE:(h + 1) * LANE], 0.0) * wgt
        for c in range(i + 1):
            s_c = sc[:, c * LANE:(c + 1) * LANE]
            if c == i:
                s_c = jnp.where(diag, s_c, -jnp.inf)
            key_ref[offs[i] + c] = _sort_key(s_c)
        tau_ref[i] = jnp.full((LANE, LANE), INT_MIN, I32)

    def bit_step(it, carry):
        bit = jnp.left_shift(jnp.int32(1), 31 - it)
        for i in range(nq):
            cand = tau_ref[i] + bit
            cnt = _count_rows([key_ref[offs[i] + c] >= cand for c in range(i + 1)], ones)
            tau_ref[i] = jnp.where(cnt >= n_sel, cand, tau_ref[i])
        return carry

    lax.fori_loop(0, 32, bit_step, 0)

    neg_blk = jnp.full((LANE, LANE), NEG, BF16)
    for i in range(nq):
        tau = tau_ref[i]
        keys = [key_ref[offs[i] + c] for c in range(i + 1)]
        need = n_sel - _count_rows([k > tau for k in keys], ones)
        bias = _selection_bias(keys, tau, need, u_ones, diag)
        for c in range(nq):
            blk = bias[c].astype(BF16) if c <= i else neg_blk
            out_ref[0, i * LANE:(i + 1) * LANE, c * LANE:(c + 1) * LANE] = blk


def _select_p(aiq, aik, omisc, b, lp, n_sel):
    nq = lp // LANE
    return pl.pallas_call(
        functools.partial(_select_p_kernel, n_sel),
        grid=(b,),
        in_specs=[pl.BlockSpec((lp, IDX_HEADS * LANE), lambda i: (i, 0)),
                  pl.BlockSpec((lp, LANE), lambda i: (i, 0)),
                  pl.BlockSpec((lp, LANE), lambda i: (i, 0))],
        out_specs=pl.BlockSpec((1, lp, lp), lambda i: (i, 0, 0)),
        out_shape=jax.ShapeDtypeStruct((b, lp, lp), BF16),
        scratch_shapes=[pltpu.VMEM((nq * (nq + 1) // 2, LANE, LANE), I32),
                        pltpu.VMEM((nq, LANE, LANE), I32)],
        compiler_params=pltpu.CompilerParams(dimension_semantics=("arbitrary",),
                                             vmem_limit_bytes=VMEM_LIMIT),
        name="select_p",
    )(aiq, aik, omisc)


def _alibi_slope(head):
    return 2.0 ** (-(8.0 / DSA_HEADS) * (head + 1))


def _attn_p_body(w, i, fq_ref, fk_ref, fv_ref, dq_ref, dkv_ref, cq_ref, ck_ref, mb_ref, of_ref, od_ref):
    tq = fq_ref.shape[0]
    lane = lax.broadcasted_iota(I32, (tq, LANE), 1)
    lo_half = lane < HEAD_DIM
    qpos = i * tq + lax.broadcasted_iota(I32, (tq, w), 0)
    kpos = lax.broadcasted_iota(I32, (tq, w), 1)
    causal = kpos <= qpos

    for j in range(FOX_HEADS // 2):
        cols = slice(j * LANE, (j + 1) * LANE)
        qp = fq_ref[:, cols]
        q2 = jnp.concatenate([jnp.where(lo_half, qp, 0), jnp.where(lo_half, 0, qp)], axis=0)
        s = _nt_dot(q2, fk_ref[0:w, cols])
        bias = []
        for h in (2 * j, 2 * j + 1):
            bias.append(cq_ref[0, :, h:h + 1] - ck_ref[0, h:h + 1, 0:w])
        s = s + jnp.concatenate(bias, axis=0)
        s = jnp.where(jnp.concatenate([causal, causal], axis=0), s, NEG)
        o = _softmax_pv(s, None, fv_ref[0:w, cols])
        of_ref[:, cols] = jnp.where(lo_half, o[:tq], o[tq:]).astype(BF16)

    dist = (qpos - kpos).astype(F32)
    mb = mb_ref[0, :, 0:w].astype(F32)
    outs = []
    for g in range(DSA_KV_HEADS):
        keep = lo_half if g == 0 else jnp.logical_not(lo_half)
        q4 = jnp.concatenate(
            [jnp.where(keep, dq_ref[:, r * LANE:(r + 1) * LANE], 0) for r in range(DSA_GROUP)], axis=0)
        s = _nt_dot(q4, dkv_ref[0:w, 0:LANE])
        bias = [mb - _alibi_slope(g * DSA_GROUP + r) * dist for r in range(DSA_GROUP)]
        s = s + jnp.concatenate(bias, axis=0)
        outs.append(_softmax_pv(s, None, dkv_ref[0:w, LANE:2 * LANE]))
    for r in range(DSA_GROUP):
        rows = slice(r * tq, (r + 1) * tq)
        od_ref[:, r * LANE:(r + 1) * LANE] = jnp.where(lo_half, outs[0][rows], outs[1][rows]).astype(BF16)


def _attn_p_kernel(buckets, *refs):
    i = pl.program_id(1)
    for lo, hi, w in buckets:
        @pl.when((i >= lo) & (i < hi))
        def _():
            _attn_p_body(w, i, *refs)


def _attn_buckets(nq):
    step = 4
    return tuple((lo, min(lo + step, nq), min(lo + step, nq) * LANE) for lo in range(0, nq, step))


def _attn_p(afq, afk, afv, adq, adkv, c_col, c_t, mb, b, lp):
    nq = lp // LANE
    qblk = lambda c: pl.BlockSpec((LANE, c), lambda bi, i: (bi * nq + i, 0))
    seq = lambda c: pl.BlockSpec((lp, c), lambda bi, i: (bi, 0))
    out = jax.ShapeDtypeStruct((b * lp, FOX_WIDTH), BF16)
    return pl.pallas_call(
        functools.partial(_attn_p_kernel, _attn_buckets(nq)),
        grid=(b, nq),
        in_specs=[qblk(FOX_WIDTH), seq(FOX_WIDTH), seq(FOX_WIDTH), qblk(DSA_WIDTH), seq(2 * DSA_KV_WIDTH),
                  pl.BlockSpec((1, LANE, FOX_HEADS), lambda bi, i: (bi, i, 0)),
                  pl.BlockSpec((1, FOX_HEADS, lp), lambda bi, i: (bi, 0, 0)),
                  pl.BlockSpec((1, LANE, lp), lambda bi, i: (bi, i, 0))],
        out_specs=[qblk(FOX_WIDTH), qblk(DSA_WIDTH)],
        out_shape=[out, out],
        compiler_params=pltpu.CompilerParams(dimension_semantics=("arbitrary", "arbitrary"),
                                             vmem_limit_bytes=VMEM_LIMIT),
        name="attn_p",
    )(afq, afk, afv, adq, adkv, c_col, c_t, mb)


def _select_s_kernel(layer, n_sel, group, pt_ref, cache_ref, iq_ref, misc_ref, out_ref,
                     ik_buf, key_ref, sem):
    n_pages = pt_ref.shape[1]
    past = n_pages * PAGE_SIZE
    t_new = iq_ref.shape[1]
    pw = past + LANE
    step = pl.program_id(0)

    def page_copy(g, p):
        page = pt_ref[step * group + g, p]
        return pltpu.make_async_copy(cache_ref.at[layer, page],
                                     ik_buf.at[g, :, pl.ds(p * PAGE_SIZE, PAGE_SIZE)], sem)

    for g in range(group):
        for p in range(n_pages):
            page_copy(g, p).start()
    for g in range(group):
        for p in range(n_pages):
            page_copy(g, p).wait()

    col = lax.broadcasted_iota(I32, (t_new, pw), 1)
    tok = lax.broadcasted_iota(I32, (t_new, pw), 0)
    visible = (col < past) | (col - past <= tok)
    for g in range(group):
        qf = iq_ref[g]
        q = jnp.concatenate([qf[:, h * LANE:h * LANE + IDX_DIM] for h in range(IDX_HEADS)], axis=0).astype(BF16)
        k_new = jnp.concatenate([misc_ref[g][:, 0:IDX_DIM],
                                 jnp.zeros((LANE - t_new, IDX_DIM), F32)], axis=0).astype(BF16)
        d = jnp.concatenate([_dot(q, ik_buf[g].astype(BF16)), _nt_dot(q, k_new)], axis=1)
        sc = jnp.zeros((t_new, pw), F32)
        for h in range(IDX_HEADS):
            wgt = misc_ref[g][:, M_IW + h:M_IW + h + 1]
            sc = sc + jnp.maximum(d[h * t_new:(h + 1) * t_new], 0.0) * wgt
        key_ref[g * t_new:(g + 1) * t_new, :] = _sort_key(jnp.where(visible, sc, -jnp.inf))

    rows = group * t_new
    nch = pw // LANE
    u_ones, ones = _prefix_consts()
    keys = [key_ref[:, c * LANE:(c + 1) * LANE] for c in range(nch)]

    def bit_step(it, tau):
        cand = tau + jnp.left_shift(jnp.int32(1), 31 - it)
        cnt = _count_rows([k >= cand for k in keys], ones)
        return jnp.where(cnt >= n_sel, cand, tau)

    tau = lax.fori_loop(0, 32, bit_step, jnp.full((rows, LANE), INT_MIN, I32))
    need = n_sel - _count_rows([k > tau for k in keys], ones)
    bias = _selection_bias(keys, tau, need, u_ones, None)
    bias = jnp.where(jnp.concatenate([visible] * group, axis=0), jnp.concatenate(bias, axis=1), NEG)
    for g in range(group):
        out_ref[g] = bias[g * t_new:(g + 1) * t_new]


def _select_s(page_table, cache_ik, iq_s, misc_s, layer, n_sel, group):
    db, t_new, _ = iq_s.shape
    n_pages = page_table.shape[1]
    past = n_pages * PAGE_SIZE
    pw = past + LANE
    blk = lambda c: pl.BlockSpec((group, t_new, c), lambda i, pt: (i, 0, 0))
    return pl.pallas_call(
        functools.partial(_select_s_kernel, layer, n_sel, group),
        grid_spec=pltpu.PrefetchScalarGridSpec(
            num_scalar_prefetch=1, grid=(db // group,),
            in_specs=[pl.BlockSpec(memory_space=pl.ANY), blk(IDX_HEADS * LANE), blk(LANE)],
            out_specs=blk(pw),
            scratch_shapes=[pltpu.VMEM((group, IDX_DIM, past), F32),
                            pltpu.VMEM((group * t_new, pw), I32),
                            pltpu.SemaphoreType.DMA(())]),
        out_shape=jax.ShapeDtypeStruct((db, t_new, pw), F32),
        compiler_params=pltpu.CompilerParams(dimension_semantics=("arbitrary",),
                                             vmem_limit_bytes=VMEM_LIMIT),
        name="select_s",
    )(page_table, cache_ik, iq_s, misc_s)


def _pad_rows(x, n):
    return jnp.concatenate([x, jnp.zeros((n - x.shape[0], x.shape[1]), x.dtype)], axis=0)


def _attn_s_kernel(layer, pt_ref, cfk_ref, cfv_ref, clf_ref, cdk_ref, cdv_ref,
                   fq_ref, dq_ref, fk_ref, fv_ref, dkv_ref, lf_ref, mb_ref, of_ref, od_ref,
                   kbuf, vbuf, lfbuf, dkbuf, dvbuf, sems):
    n_pages = pt_ref.shape[1]
    past = n_pages * PAGE_SIZE
    t_new = fq_ref.shape[1]
    pw = past + LANE
    step = pl.program_id(0)
    nsteps = pl.num_programs(0)
    slot = step % 2

    def copies(seq, sl):
        out = []
        for p in range(n_pages):
            page = pt_ref[seq, p]
            keys = pl.ds(p * PAGE_SIZE, PAGE_SIZE)
            out.append(pltpu.make_async_copy(cfk_ref.at[layer, page], kbuf.at[sl, :, keys], sems.at[sl, 0]))
            out.append(pltpu.make_async_copy(cfv_ref.at[layer, page], vbuf.at[sl, :, keys], sems.at[sl, 1]))
            out.append(pltpu.make_async_copy(clf_ref.at[layer, page], lfbuf.at[sl, :, keys], sems.at[sl, 2]))
            out.append(pltpu.make_async_copy(cdk_ref.at[layer, page], dkbuf.at[sl, :, keys], sems.at[sl, 3]))
            out.append(pltpu.make_async_copy(cdv_ref.at[layer, page], dvbuf.at[sl, :, keys], sems.at[sl, 4]))
        return out

    @pl.when(step == 0)
    def _():
        for c in copies(step, slot):
            c.start()

    @pl.when(step + 1 < nsteps)
    def _():
        for c in copies(step + 1, 1 - slot):
            c.start()

    for c in copies(step, slot):
        c.wait()

    rows = t_new * FOX_HEADS
    r_idx = lax.broadcasted_iota(I32, (rows, pw), 0)
    col = lax.broadcasted_iota(I32, (rows, pw), 1)

    tok = r_idx // FOX_HEADS
    visible = (col < past) | (col - past <= tok)
    lane_w = lax.broadcasted_iota(I32, (rows, FOX_WIDTH), 1)
    row_w = lax.broadcasted_iota(I32, (rows, FOX_WIDTH), 0)
    head_mask = (lane_w // HEAD_DIM) == (row_w % FOX_HEADS)
    q = fq_ref[0]
    q_bd = jnp.where(head_mask, jnp.broadcast_to(q[:, None, :], (t_new, FOX_HEADS, FOX_WIDTH)).reshape(rows, FOX_WIDTH), 0.0)
    q_bd = q_bd.astype(BF16)
    k_past_t = kbuf[slot].astype(BF16)
    v_past_t = vbuf[slot].astype(BF16)
    k_new = _pad_rows(fk_ref[0], LANE).astype(BF16)
    v_new = _pad_rows(fv_ref[0], LANE).astype(BF16)
    s = jnp.concatenate([_dot(q_bd, k_past_t), _nt_dot(q_bd, k_new)], axis=1)

    u = _upper_tri(LANE)
    carry = jnp.zeros((FOX_HEADS, 1), F32)
    cks = []
    for p in range(n_pages):
        out = _cumsum_lanes(lfbuf[slot, :, p * PAGE_SIZE:(p + 1) * PAGE_SIZE], u, carry)
        cks.append(out)
        carry = out[:, LANE - 1:LANE]
    c_new = _cumsum_lanes(lf_ref[0], u, carry)
    cks.append(c_new)
    ck = jnp.concatenate(cks, axis=1)
    ck_rows = jnp.concatenate([ck] * t_new, axis=0)
    c_new_rows = jnp.concatenate([c_new] * t_new, axis=0)
    pick = lax.broadcasted_iota(I32, (rows, LANE), 1) == lax.broadcasted_iota(I32, (rows, LANE), 0) // FOX_HEADS
    cq = jnp.sum(jnp.where(pick, c_new_rows, 0.0), axis=1, keepdims=True)
    s = jnp.where(visible, s + (cq - ck_rows), NEG)
    o = _softmax_pv(s, v_past_t, v_new)
    o = jnp.where(head_mask, o, 0.0).reshape(t_new, FOX_HEADS, FOX_WIDTH)
    of_ref[0] = jnp.sum(o, axis=1)

    tok_d = r_idx % t_new
    piece = lax.broadcasted_iota(I32, (rows, 1), 0) // t_new
    head = (piece % 2) * DSA_GROUP + piece // 2
    slope = jnp.zeros((rows, 1), F32)
    for h in range(DSA_HEADS):
        slope = jnp.where(head == h, _alibi_slope(h), slope)
    dq = dq_ref[0]
    lane_b = lax.broadcasted_iota(I32, (t_new, LANE), 1)
    pieces = []
    for r in range(DSA_GROUP):
        blk = dq[:, r * LANE:(r + 1) * LANE]
        pieces.append(jnp.where(lane_b < HEAD_DIM, blk, 0.0))
        pieces.append(jnp.where(lane_b < HEAD_DIM, 0.0, blk))
    q_d = jnp.concatenate(pieces, axis=0).astype(BF16)
    dk_past_t = dkbuf[slot].astype(BF16)
    dv_past_t = dvbuf[slot].astype(BF16)
    dkv_new = _pad_rows(dkv_ref[0], LANE).astype(BF16)
    s = jnp.concatenate([_dot(q_d, dk_past_t), _nt_dot(q_d, dkv_new[:, 0:LANE])], axis=1)
    mb = jnp.concatenate([mb_ref[0]] * (rows // t_new), axis=0)
    dist = (past + tok_d - col).astype(F32)
    s = s + (mb - slope * dist)
    o = _softmax_pv(s, dv_past_t, dkv_new[:, LANE:2 * LANE])
    lane_o = lax.broadcasted_iota(I32, (t_new, LANE), 1)
    blocks = []
    for r in range(DSA_GROUP):
        a = o[(2 * r) * t_new:(2 * r + 1) * t_new]
        b = o[(2 * r + 1) * t_new:(2 * r + 2) * t_new]
        blocks.append(jnp.where(lane_o < HEAD_DIM, a, b))
    od_ref[0] = jnp.concatenate(blocks, axis=1)


def _attn_s(page_table, cfk, cfv, clf_t, cdk, cdv, fq_s, dq_s, fk_s, fv_s, dkv_s, lf_s, mb_s, layer):
    db, t_new, _ = fq_s.shape
    n_pages = page_table.shape[1]
    past = n_pages * PAGE_SIZE
    pw = past + LANE
    blk = lambda r, c: pl.BlockSpec((1, r, c), lambda i, pt: (i, 0, 0))
    anyspec = pl.BlockSpec(memory_space=pl.ANY)
    out = jax.ShapeDtypeStruct((db, t_new, FOX_WIDTH), F32)
    return pl.pallas_call(
        functools.partial(_attn_s_kernel, layer),
        grid_spec=pltpu.PrefetchScalarGridSpec(
            num_scalar_prefetch=1, grid=(db,),
            in_specs=[anyspec] * 5 + [blk(t_new, FOX_WIDTH), blk(t_new, DSA_WIDTH), blk(t_new, FOX_WIDTH),
                                      blk(t_new, FOX_WIDTH), blk(t_new, 2 * DSA_KV_WIDTH),
                                      blk(FOX_HEADS, LANE), blk(t_new, pw)],
            out_specs=[blk(t_new, FOX_WIDTH), blk(t_new, DSA_WIDTH)],
            scratch_shapes=[pltpu.VMEM((2, FOX_WIDTH, past), F32),
                            pltpu.VMEM((2, FOX_WIDTH, past), F32),
                            pltpu.VMEM((2, FOX_HEADS, past), F32),
                            pltpu.VMEM((2, DSA_KV_WIDTH, past), F32),
                            pltpu.VMEM((2, DSA_KV_WIDTH, past), F32),
                            pltpu.SemaphoreType.DMA((2, 5))]),
        out_shape=[out, out],
        compiler_params=pltpu.CompilerParams(dimension_semantics=("arbitrary",),
                                             vmem_limit_bytes=VMEM_LIMIT),
        name="attn_s",
    )(page_table, cfk, cfv, clf_t, cdk, cdv, fq_s, dq_s, fk_s, fv_s, dkv_s, lf_s, mb_s)


def _router_gates(s, sb):
    row = lambda a, e: a[e:e + 1, :]
    gscore = []
    for g in range(N_EXPERT_GROUPS):
        a, b, c, d = (row(sb, EXPERTS_PER_GROUP * g + k) for k in range(EXPERTS_PER_GROUP))
        m1, n1 = jnp.maximum(a, b), jnp.minimum(a, b)
        m2, n2 = jnp.maximum(c, d), jnp.minimum(c, d)
        gscore.append(jnp.maximum(m1, m2) + jnp.maximum(jnp.minimum(m1, m2), jnp.maximum(n1, n2)))
    best, gi = gscore[0], jnp.zeros(gscore[0].shape, I32)
    for g in range(1, N_EXPERT_GROUPS):
        better = gscore[g] > best
        gi = jnp.where(better, g, gi)
        best = jnp.where(better, gscore[g], best)

    def in_group(arr, k):
        out = row(arr, k)
        for g in range(1, N_EXPERT_GROUPS):
            out = jnp.where(gi == g, row(arr, EXPERTS_PER_GROUP * g + k), out)
        return out

    vb = [in_group(sb, k) for k in range(EXPERTS_PER_GROUP)]
    vs = [in_group(s, k) for k in range(EXPERTS_PER_GROUP)]

    def arg_first_max(vals):
        best, idx = vals[0], jnp.zeros(vals[0].shape, I32)
        for k in range(1, len(vals)):
            better = vals[k] > best
            idx = jnp.where(better, k, idx)
            best = jnp.where(better, vals[k], best)
        return idx

    i1 = arg_first_max(vb)
    i2 = arg_first_max([jnp.where(i1 == k, -jnp.inf, vb[k]) for k in range(EXPERTS_PER_GROUP)])

    def take(vals, idx):
        out = vals[0]
        for k in range(1, len(vals)):
            out = jnp.where(idx == k, vals[k], out)
        return out

    g1, g2 = take(vs, i1), take(vs, i2)
    den = g1 + g2
    g1, g2 = g1 / den, g2 / den
    e1 = gi * EXPERTS_PER_GROUP + i1
    e2 = gi * EXPERTS_PER_GROUP + i2
    rows = [jnp.where(e1 == e, g1, jnp.where(e2 == e, g2, 0.0)) for e in range(N_EXPERTS)]
    return jnp.concatenate(rows, axis=0)


def _outproj_kernel(alpha, n_prompt_tiles, x_ref, mfp_ref, mdp_ref, mfs_ref, mds_ref, wf_ref, wd_ref,
                    g_ref, b_ref, wr_ref, br_ref, x1_ref, gates_ref):
    is_prompt = pl.program_id(0) < n_prompt_tiles
    mf = jnp.where(is_prompt, mfp_ref[...], mfs_ref[...])
    md = jnp.where(is_prompt, mdp_ref[...], mds_ref[...])
    y = _dot(mf, wf_ref[...]) + _dot(md, wd_ref[...])
    x1 = _layer_norm(alpha * x_ref[...] + y, g_ref[...], b_ref[...])
    x1_ref[...] = x1
    xh, xm, _ = _split3(x1)
    wh, wm, _ = _split3(wr_ref[...])
    logits = _nt_dot(wh, xh) + (_nt_dot(wh, xm) + _nt_dot(wm, xh))
    s = 1.0 / (1.0 + jnp.exp(-logits))
    gates_ref[...] = _router_gates(s, s + br_ref[...])


def _outproj(x, mix_fp, mix_dp, mix_fs, mix_ds, wo_f, wo_d, g, b, wr_t, br, layer, alpha, tm):
    rt, d = x.shape
    npt = mix_fp.shape[0] // tm
    assert mix_fp.shape[0] % tm == 0 and mix_fs.shape[0] % tm == 0
    row = lambda c: pl.BlockSpec((tm, c), lambda i: (i, 0))
    prow = lambda c: pl.BlockSpec((tm, c), lambda i: (jnp.minimum(i, npt - 1), 0))
    srow = lambda c: pl.BlockSpec((tm, c), lambda i: (jnp.maximum(i - npt, 0), 0))
    lay = lambda r, c: pl.BlockSpec((None, r, c), lambda i: (layer, 0, 0))
    full = lambda r, c: pl.BlockSpec((r, c), lambda i: (0, 0))
    return pl.pallas_call(
        functools.partial(_outproj_kernel, alpha, npt),
        grid=(rt // tm,),
        in_specs=[row(d), prow(FOX_WIDTH), prow(DSA_WIDTH), srow(FOX_WIDTH), srow(DSA_WIDTH),
                  lay(FOX_WIDTH, d), lay(DSA_WIDTH, d),
                  lay(1, d), lay(1, d), full(N_EXPERTS, d), full(N_EXPERTS, 1)],
        out_specs=[row(d), pl.BlockSpec((N_EXPERTS, tm), lambda i: (0, i))],
        out_shape=[jax.ShapeDtypeStruct((rt, d), F32), jax.ShapeDtypeStruct((N_EXPERTS, rt), F32)],
        compiler_params=pltpu.CompilerParams(dimension_semantics=("arbitrary",),
                                             vmem_limit_bytes=VMEM_LIMIT),
        name="outproj",
    )(x, mix_fp, mix_dp, mix_fs, mix_ds, wo_f, wo_d, g, b, wr_t, br)


def _moe_kernel(alpha, x_ref, gate_ref, wg_ref, wu_ref, wd_ref, g_ref, b_ref, out_ref, acc_ref):
    j = pl.program_id(1)

    @pl.when(j == 0)
    def _():
        acc_ref[...] = jnp.zeros(acc_ref.shape, F32)

    xb = x_ref[...].astype(BF16)
    acc = acc_ref[...]
    for e in range(EXPERTS_PER_GROUP):
        hg = _dot(xb, wg_ref[e])
        hu = _dot(xb, wu_ref[e])
        a = (hg * (1.0 / (1.0 + jnp.exp(-hg)))) * hu * gate_ref[:, e:e + 1]
        acc = acc + _dot(a.astype(BF16), wd_ref[e])
    acc_ref[...] = acc

    @pl.when(j == pl.num_programs(1) - 1)
    def _():
        out_ref[...] = _layer_norm(alpha * x_ref[...] + acc_ref[...], g_ref[...], b_ref[...])


def _moe(x1, gates_g, wg, wu, wd, g, b, layer, alpha, tm):
    rt, d = x1.shape
    epg = EXPERTS_PER_GROUP
    return pl.pallas_call(
        functools.partial(_moe_kernel, alpha),
        grid=(rt // tm, N_EXPERT_GROUPS),
        in_specs=[pl.BlockSpec((tm, d), lambda i, j: (i, 0)),
                  pl.BlockSpec((None, tm, epg), lambda i, j: (j, i, 0)),
                  pl.BlockSpec((None, epg, d, D_EXPERT), lambda i, j: (layer, j, 0, 0)),
                  pl.BlockSpec((None, epg, d, D_EXPERT), lambda i, j: (layer, j, 0, 0)),
                  pl.BlockSpec((None, epg, D_EXPERT, d), lambda i, j: (layer, j, 0, 0)),
                  pl.BlockSpec((None, 1, d), lambda i, j: (layer, 0, 0)),
                  pl.BlockSpec((None, 1, d), lambda i, j: (layer, 0, 0))],
        out_specs=pl.BlockSpec((tm, d), lambda i, j: (i, 0)),
        out_shape=jax.ShapeDtypeStruct((rt, d), F32),
        scratch_shapes=[pltpu.VMEM((tm, d), F32)],
        compiler_params=pltpu.CompilerParams(dimension_semantics=("arbitrary", "arbitrary"),
                                             vmem_limit_bytes=VMEM_LIMIT),
        name="moe",
    )(x1, gates_g, wg, wu, wd, g, b)


def _pack_w_in(w_in, b_forget):
    depth, d, _ = w_in.shape
    o = 0
    seg = {}
    for name, size in (("fq", FOX_WIDTH), ("fk", FOX_WIDTH), ("fv", FOX_WIDTH), ("ff", FOX_HEADS),
                       ("dq", DSA_WIDTH), ("dk", DSA_KV_WIDTH), ("dv", DSA_KV_WIDTH),
                       ("iq", IDX_HEADS * IDX_DIM), ("ik", IDX_DIM), ("iw", IDX_HEADS)):
        seg[name] = w_in[:, :, o:o + size]
        o += size
    head_order = [h for r in range(DSA_GROUP) for h in (r, DSA_GROUP + r)]
    dq = seg["dq"].reshape(depth, d, DSA_HEADS, HEAD_DIM)[:, :, head_order].reshape(depth, d, DSA_WIDTH)
    iq = jnp.pad(seg["iq"].reshape(depth, d, IDX_HEADS, IDX_DIM), ((0, 0), (0, 0), (0, 0), (0, LANE - IDX_DIM)))
    iq = iq.reshape(depth, d, IDX_HEADS * LANE)
    misc = jnp.concatenate([seg["ik"], seg["ff"], seg["iw"],
                            jnp.zeros((depth, d, LANE - M_IW - IDX_HEADS), w_in.dtype)], axis=2)
    w = jnp.concatenate([seg["fq"], seg["fk"], seg["fv"], dq, seg["dk"], seg["dv"], iq, misc], axis=2)
    bf = jnp.pad(b_forget.astype(F32), ((0, 0), (M_LOGF, LANE - M_IW)))[:, None, :]
    return w.astype(BF16), bf, head_order


def _row_tile(rt, cap):
    t = cap
    while rt % t:
        t //= 2
    return t


def kernel(x_prompt, x_sample, cache_fox_k, cache_fox_v, cache_fox_logf, cache_dsa_k, cache_dsa_v,
           cache_idx_k, page_table, meta_tokens, w_in, b_forget, w_out, ln1_g, ln1_b, ln2_g, ln2_b,
           w_router, b_router, w_gate, w_up, w_down):
    b, seq, d = x_prompt.shape
    db, t_new, _ = x_sample.shape
    depth = w_in.shape[0]
    n_pool = cache_fox_k.shape[1]
    n_pages = page_table.shape[1]
    past = n_pages * PAGE_SIZE
    l = seq + N_META
    lp = -(-l // LANE) * LANE
    rp = b * lp
    rt = rp + db * t_new
    alpha = (2.0 * depth) ** 0.25
    n_sel_p = min(MAX_SELECT, (l - N_META) // 4)
    n_sel_s = min(MAX_SELECT, (past + t_new) // 4)
    group_s = min(16, db)

    w_packed, bf_packed, head_order = _pack_w_in(w_in, b_forget)
    wo_f = w_out[:, :FOX_WIDTH].astype(BF16)
    wo_d = w_out[:, FOX_WIDTH:].reshape(depth, DSA_HEADS, HEAD_DIM, d)[:, head_order].reshape(depth, DSA_WIDTH, d)
    wo_d = wo_d.astype(BF16)
    wr_t = w_router.T.astype(F32)
    br = b_router.astype(F32)[:, None]
    wg, wu, wd = w_gate.astype(BF16), w_up.astype(BF16), w_down.astype(BF16)
    ln = lambda a: a.astype(F32)[:, None, :]
    g1, b1, g2, b2 = ln(ln1_g), ln(ln1_b), ln(ln2_g), ln(ln2_b)
    keys_minor = lambda c, width: jnp.moveaxis(c, 2, -1).reshape(depth, n_pool, width, PAGE_SIZE)
    cfk = keys_minor(cache_fox_k, FOX_WIDTH)
    cfv = keys_minor(cache_fox_v, FOX_WIDTH)
    clf_t = keys_minor(cache_fox_logf, FOX_HEADS)
    cdk = keys_minor(cache_dsa_k, DSA_KV_WIDTH)
    cdv = keys_minor(cache_dsa_v, DSA_KV_WIDTH)
    cik = keys_minor(cache_idx_k, IDX_DIM)

    meta = jnp.broadcast_to(meta_tokens[None].astype(F32), (b, N_META, d))
    xp = jnp.concatenate([meta, x_prompt, jnp.zeros((b, lp - l, d), F32)], axis=1)
    x = jnp.concatenate([xp.reshape(rp, d), x_sample.reshape(db * t_new, d)], axis=0)

    tm = _row_tile(rt, 512)
    tm_moe = _row_tile(rt, 1024)
    rows_p = [[] for _ in range(6)]
    rows_s = [[] for _ in range(6)]
    for layer in range(depth):
        ofk, ofv, odkv, omisc, afq, afk, afv, adq, adkv, aiq, aik = _proj(x, w_packed, bf_packed, layer, tm)

        lf_t = jnp.swapaxes(omisc[:rp, M_LOGF:M_IW].reshape(b, lp, FOX_HEADS), 1, 2)
        c_t = _cumsum(lf_t)
        c_col = jnp.swapaxes(c_t, 1, 2)
        mb_p = _select_p(aiq, aik, omisc, b, lp, n_sel_p)
        mf_p, md_p = _attn_p(afq, afk, afv, adq, adkv, c_col, c_t, mb_p, b, lp)

        smp = lambda a: a[rp:].reshape(db, t_new, a.shape[1]).astype(F32)
        misc_s = smp(omisc)
        mb_s = _select_s(page_table, cik, smp(aiq), misc_s, layer, n_sel_s, group_s)
        lf_s = jnp.pad(jnp.swapaxes(misc_s[:, :, M_LOGF:M_IW], 1, 2), ((0, 0), (0, 0), (0, LANE - t_new)))
        mf_s, md_s = _attn_s(page_table, cfk, cfv, clf_t, cdk, cdv, smp(afq), smp(adq), smp(ofk), smp(ofv),
                             smp(odkv), lf_s, mb_s, layer)

        mf_s = mf_s.reshape(db * t_new, FOX_WIDTH).astype(BF16)
        md_s = md_s.reshape(db * t_new, DSA_WIDTH).astype(BF16)
        x1, gates_t = _outproj(x, mf_p, md_p, mf_s, md_s, wo_f, wo_d, g1, b1, wr_t, br, layer, alpha, tm)
        gates_g = jnp.swapaxes(gates_t.reshape(N_EXPERT_GROUPS, EXPERTS_PER_GROUP, rt), 1, 2)
        x = _moe(x1, gates_g, wg, wu, wd, g2, b2, layer, alpha, tm_moe)

        new = (ofk, ofv, omisc[:, M_LOGF:M_IW], odkv[:, :DSA_KV_WIDTH], odkv[:, DSA_KV_WIDTH:], omisc[:, :IDX_DIM])
        for k, a in enumerate(new):
            rows_p[k].append(a[:rp].reshape(b, lp, -1)[:, :l])
            rows_s[k].append(a[rp:].reshape(db, t_new, -1))

    y_prompt = x[:rp].reshape(b, lp, d)[:, N_META:l]
    y_sample = x[rp:].reshape(db, t_new, d)
    shapes = [(FOX_HEADS, HEAD_DIM), (FOX_HEADS, HEAD_DIM), (FOX_HEADS,), (DSA_KV_HEADS, HEAD_DIM),
              (DSA_KV_HEADS, HEAD_DIM), (IDX_DIM,)]
    outs_p = [jnp.stack(r).reshape((depth, b, l) + s) for r, s in zip(rows_p, shapes)]
    outs_s = [jnp.stack(r).reshape((depth, db, t_new) + s) for r, s in zip(rows_s, shapes)]
    return (y_prompt, y_sample, *outs_p, *outs_s)
```

```python
import functools

import jax
import jax.numpy as jnp
from jax import lax
from jax.experimental import pallas as pl
from jax.experimental.pallas import tpu as pltpu

F32 = jnp.float32
BF16 = jnp.bfloat16
I32 = jnp.int32

HEAD_DIM = 64
FOX_HEADS = 8
DSA_HEADS = 8
DSA_KV_HEADS = 2
DSA_GROUP = DSA_HEADS // DSA_KV_HEADS
IDX_HEADS = 4
IDX_DIM = 64
MAX_SELECT = 256
N_META = 16
PAGE_SIZE = 128
N_EXPERTS = 16
N_EXPERT_GROUPS = 4
EXPERTS_PER_GROUP = N_EXPERTS // N_EXPERT_GROUPS
D_EXPERT = 256
LN_EPS = 1e-5

LANE = 128
FOX_WIDTH = FOX_HEADS * HEAD_DIM
DSA_WIDTH = DSA_HEADS * HEAD_DIM
DSA_KV_WIDTH = DSA_KV_HEADS * HEAD_DIM
C_FQ = 0
C_FK = C_FQ + FOX_WIDTH
C_FV = C_FK + FOX_WIDTH
C_DQ = C_FV + FOX_WIDTH
C_DKV = C_DQ + DSA_WIDTH
C_IQ = C_DKV + 2 * DSA_KV_WIDTH
C_MISC = C_IQ + IDX_HEADS * LANE
N_PACKED = C_MISC + LANE
M_LOGF = IDX_DIM
M_IW = M_LOGF + FOX_HEADS

NEG = -1e30
LOG2E = 1.4426950408889634
Q_SCALE = HEAD_DIM ** -0.5 * LOG2E
INT_MIN = -(2 ** 31)
VMEM_LIMIT = 56 * 1024 * 1024


def _nt_dot(a, b):
    return lax.dot_general(a, b, (((1,), (1,)), ((), ())), preferred_element_type=F32)


def _dot(a, b):
    return jnp.dot(a, b, preferred_element_type=F32)


def _split3(x):
    hi = x.astype(BF16)
    r1 = x - hi.astype(F32)
    mid = r1.astype(BF16)
    lo = (r1 - mid.astype(F32)).astype(BF16)
    return hi, mid, lo


def _upper_tri(n):
    r = lax.broadcasted_iota(I32, (n, n), 0)
    c = lax.broadcasted_iota(I32, (n, n), 1)
    return jnp.where(r <= c, 1.0, 0.0).astype(BF16)


def _layer_norm(z, g, b):
    mu = jnp.mean(z, axis=-1, keepdims=True)
    zc = z - mu
    var = jnp.mean(zc * zc, axis=-1, keepdims=True)
    return zc * lax.rsqrt(var + LN_EPS) * g + b


def _sort_key(x):
    x = jnp.where(x == 0.0, 0.0, x)
    b = lax.bitcast_convert_type(x, I32)
    return b ^ ((b >> 31) & 0x7FFFFFFF)


def _softmax_pv(s, v_t, v_new):
    m = jnp.max(s, axis=1, keepdims=True)
    p = jnp.exp2(s - m)
    l = jnp.sum(p, axis=1, keepdims=True)
    pb = p.astype(BF16)
    w0 = v_t.shape[1]
    return (_nt_dot(pb[:, :w0], v_t) + _dot(pb[:, w0:], v_new)) / l


def _softmax_num(s):
    return jnp.exp2(s - jnp.max(s, axis=1, keepdims=True)).astype(BF16)


def _pv_ones(p, v_ones):
    o = _dot(p, v_ones)
    return o[:, :LANE] / o[:, LANE:]


def _proj_kernel(x_ref, w_ref, bf_ref, ofk, ofv, odkv, omisc, afq, afk, afv, adq, adkv, aiq, aik):
    x = x_ref[...].astype(BF16)

    def seg(a, b):
        return _dot(x, w_ref[:, a:b])

    afq[...] = (seg(C_FQ, C_FK) * Q_SCALE).astype(BF16)
    fk = seg(C_FK, C_FV)
    ofk[...] = fk
    afk[...] = fk.astype(BF16)
    fv = seg(C_FV, C_DQ)
    ofv[...] = fv
    afv[...] = fv.astype(BF16)
    adq[...] = (seg(C_DQ, C_DKV) * Q_SCALE).astype(BF16)
    dkv = seg(C_DKV, C_IQ)
    odkv[...] = dkv
    adkv[...] = dkv.astype(BF16)
    aiq[...] = seg(C_IQ, C_MISC).astype(BF16)
    m = seg(C_MISC, N_PACKED)
    lane = lax.broadcasted_iota(I32, m.shape, 1)
    z = m + bf_ref[...]
    logf = jnp.minimum(z, 0.0) - jnp.log1p(jnp.exp(-jnp.abs(z)))
    iw = m * (IDX_HEADS ** -0.5 * IDX_DIM ** -0.5)
    is_f = (lane >= M_LOGF) & (lane < M_IW)
    is_w = (lane >= M_IW) & (lane < M_IW + IDX_HEADS)
    omisc[...] = jnp.where(is_f, logf, jnp.where(is_w, iw, m))
    aik[...] = m.astype(BF16)


def _proj(x, w_packed, bf_packed, layer, tm):
    rt, d = x.shape
    row = lambda c: pl.BlockSpec((tm, c), lambda i: (i, 0))
    outs = [(FOX_WIDTH, F32), (FOX_WIDTH, F32), (2 * DSA_KV_WIDTH, F32), (LANE, F32),
            (FOX_WIDTH, BF16), (FOX_WIDTH, BF16), (FOX_WIDTH, BF16), (DSA_WIDTH, BF16),
            (2 * DSA_KV_WIDTH, BF16), (IDX_HEADS * LANE, BF16), (LANE, BF16)]
    return pl.pallas_call(
        _proj_kernel,
        grid=(rt // tm,),
        in_specs=[row(d),
                  pl.BlockSpec((None, d, N_PACKED), lambda i: (layer, 0, 0)),
                  pl.BlockSpec((None, 1, LANE), lambda i: (layer, 0, 0))],
        out_specs=[row(c) for c, _ in outs],
        out_shape=[jax.ShapeDtypeStruct((rt, c), dt) for c, dt in outs],
        compiler_params=pltpu.CompilerParams(dimension_semantics=("arbitrary",),
                                             vmem_limit_bytes=VMEM_LIMIT),
        name="proj",
    )(x, w_packed, bf_packed)


def _cumsum_lanes(x, u, carry):
    hi, mid, lo = _split3(x)
    return _dot(hi, u) + _dot(mid, u) + _dot(lo, u) + carry


def _cumsum_kernel(lf_ref, c_ref):
    n = lf_ref.shape[-1] // LANE
    u = _upper_tri(LANE)
    carry = jnp.zeros((lf_ref.shape[1], 1), F32)
    for c in range(n):
        out = _cumsum_lanes(lf_ref[0, :, c * LANE:(c + 1) * LANE], u, carry)
        c_ref[0, :, c * LANE:(c + 1) * LANE] = out * LOG2E
        carry = out[:, LANE - 1:LANE]


def _cumsum(lf_t):
    b, h, lp = lf_t.shape
    spec = pl.BlockSpec((1, h, lp), lambda i: (i, 0, 0))
    return pl.pallas_call(
        _cumsum_kernel, grid=(b,), in_specs=[spec], out_specs=spec,
        out_shape=jax.ShapeDtypeStruct(lf_t.shape, F32),
        compiler_params=pltpu.CompilerParams(dimension_semantics=("arbitrary",)),
        name="cumsum",
    )(lf_t)


def _count_rows(pred_chunks, ones):
    acc = None
    for p in pred_chunks:
        t = jnp.where(p, 1.0, 0.0)
        acc = t if acc is None else acc + t
    return _dot(acc.astype(BF16), ones)


def _selection_bias(keys, tau, need, u_ones, causal_last):
    out = []
    carry = jnp.zeros(tau.shape, F32)
    for c, k in enumerate(keys):
        eq = k == tau
        eqf = jnp.where(eq, 1.0, 0.0)
        pr = _dot(eqf.astype(BF16), u_ones)
        rank = carry + pr[:, :LANE] - eqf
        b = jnp.where(k > tau, 0.0, jnp.where(eq, jnp.where(rank < need, 0.0, NEG), NEG))
        if c == len(keys) - 1 and causal_last is not None:
            b = jnp.where(causal_last, b, NEG)
        out.append(b)
        carry = carry + pr[:, LANE:]
    return out


def _prefix_consts():
    u = _upper_tri(LANE)
    ones = jnp.ones((LANE, LANE), BF16)
    return jnp.concatenate([u, ones], axis=1), ones


def _select_p_kernel(n_sel, iq_ref, ik_ref, misc_ref, out_ref, key_ref, tau_ref):
    lp = iq_ref.shape[0]
    nq = lp // LANE
    offs = [i * (i + 1) // 2 for i in range(nq)]
    u_ones, ones = _prefix_consts()
    rloc = lax.broadcasted_iota(I32, (LANE, LANE), 0)
    cloc = lax.broadcasted_iota(I32, (LANE, LANE), 1)
    diag = cloc <= rloc

    for i in range(nq):
        w = (i + 1) * LANE
        rows = slice(i * LANE, (i + 1) * LANE)
        q = jnp.concatenate([iq_ref[rows, h * LANE:h * LANE + IDX_DIM] for h in range(IDX_HEADS)], axis=0)
        d = _nt_dot(q, ik_ref[0:w, 0:IDX_DIM])
        sc = jnp.zeros((LANE, w), F32)
        for h in range(IDX_HEADS):
            wgt = misc_ref[rows, M_IW + h:M_IW + h + 1]
            sc = sc + jnp.maximum(d[h * LANE:(h + 1) * LANE], 0.0) * wgt
        for c in range(i + 1):
            s_c = sc[:, c * LANE:(c + 1) * LANE]
            if c == i:
                s_c = jnp.where(diag, s_c, -jnp.inf)
            key_ref[offs[i] + c] = _sort_key(s_c)
        tau_ref[i] = jnp.full((LANE, LANE), INT_MIN, I32)

    def bit_step(it, carry):
        bit = jnp.left_shift(jnp.int32(1), 31 - it)
        for i in range(nq):
            cand = tau_ref[i] + bit
            cnt = _count_rows([key_ref[offs[i] + c] >= cand for c in range(i + 1)], ones)
            tau_ref[i] = jnp.where(cnt >= n_sel, cand, tau_ref[i])
        return carry

    lax.fori_loop(0, 32, bit_step, 0)

    neg_blk = jnp.full((LANE, LANE), NEG, BF16)
    for i in range(nq):
        tau = tau_ref[i]
        keys = [key_ref[offs[i] + c] for c in range(i + 1)]
        need = n_sel - _count_rows([k > tau for k in keys], ones)
        bias = _selection_bias(keys, tau, need, u_ones, diag)
        for c in range(nq):
            blk = bias[c].astype(BF16) if c <= i else neg_blk
            out_ref[0, i * LANE:(i + 1) * LANE, c * LANE:(c + 1) * LANE] = blk


def _select_p(aiq, aik, omisc, b, lp, n_sel):
    nq = lp // LANE
    return pl.pallas_call(
        functools.partial(_select_p_kernel, n_sel),
        grid=(b,),
        in_specs=[pl.BlockSpec((lp, IDX_HEADS * LANE), lambda i: (i, 0)),
                  pl.BlockSpec((lp, LANE), lambda i: (i, 0)),
                  pl.BlockSpec((lp, LANE), lambda i: (i, 0))],
        out_specs=pl.BlockSpec((1, lp, lp), lambda i: (i, 0, 0)),
        out_shape=jax.ShapeDtypeStruct((b, lp, lp), BF16),
        scratch_shapes=[pltpu.VMEM((nq * (nq + 1) // 2, LANE, LANE), I32),
                        pltpu.VMEM((nq, LANE, LANE), I32)],
        compiler_params=pltpu.CompilerParams(dimension_semantics=("arbitrary",),
                                             vmem_limit_bytes=VMEM_LIMIT),
        name="select_p",
    )(aiq, aik, omisc)


def _alibi_slope(head):
    return 2.0 ** (-(8.0 / DSA_HEADS) * (head + 1)) * LOG2E


def _attn_p_body(w, i, fq_ref, fk_ref, fv_ref, dq_ref, dkv_ref, cq_ref, ck_ref, mb_ref, of_ref, od_ref,
                 p_ref):
    tq = fq_ref.shape[0]
    lane = lax.broadcasted_iota(I32, (tq, LANE), 1)
    lo_half = lane < HEAD_DIM
    qpos = i * tq + lax.broadcasted_iota(I32, (tq, w), 0)
    kpos = lax.broadcasted_iota(I32, (tq, w), 1)
    causal = kpos <= qpos
    ones = jnp.ones((w, LANE), BF16)

    for j in range(FOX_HEADS // 2):
        cols = slice(j * LANE, (j + 1) * LANE)
        qp = fq_ref[:, cols]
        q2 = jnp.concatenate([jnp.where(lo_half, qp, 0), jnp.where(lo_half, 0, qp)], axis=0)
        s = _nt_dot(q2, fk_ref[0:w, cols])
        for hh in range(2):
            h = 2 * j + hh
            t = s[hh * tq:(hh + 1) * tq] + (cq_ref[0, :, h:h + 1] - ck_ref[0, h:h + 1, 0:w])
            p_ref[hh * tq:(hh + 1) * tq, 0:w] = _softmax_num(jnp.where(causal, t, NEG))
        o = _pv_ones(p_ref[0:2 * tq, 0:w], jnp.concatenate([fv_ref[0:w, cols], ones], axis=1))
        of_ref[:, cols] = jnp.where(lo_half, o[:tq], o[tq:]).astype(BF16)

    dist = (qpos - kpos).astype(F32)
    mb = mb_ref[0, :, 0:w].astype(F32)
    v_ones = jnp.concatenate([dkv_ref[0:w, LANE:2 * LANE], ones], axis=1)
    outs = []
    for g in range(DSA_KV_HEADS):
        keep = lo_half if g == 0 else jnp.logical_not(lo_half)
        q4 = jnp.concatenate(
            [jnp.where(keep, dq_ref[:, r * LANE:(r + 1) * LANE], 0) for r in range(DSA_GROUP)], axis=0)
        s = _nt_dot(q4, dkv_ref[0:w, 0:LANE])
        for r in range(DSA_GROUP):
            t = s[r * tq:(r + 1) * tq] + (mb - _alibi_slope(g * DSA_GROUP + r) * dist)
            p_ref[r * tq:(r + 1) * tq, 0:w] = _softmax_num(t)
        outs.append(_pv_ones(p_ref[0:DSA_GROUP * tq, 0:w], v_ones))
    for r in range(DSA_GROUP):
        rows = slice(r * tq, (r + 1) * tq)
        od_ref[:, r * LANE:(r + 1) * LANE] = jnp.where(lo_half, outs[0][rows], outs[1][rows]).astype(BF16)


def _attn_p_kernel(buckets, *refs):
    i = pl.program_id(1)
    for lo, hi, w in buckets:
        @pl.when((i >= lo) & (i < hi))
        def _():
            _attn_p_body(w, i, *refs)


def _attn_buckets(nq):
    step = 2
    return tuple((lo, min(lo + step, nq), min(lo + step, nq) * LANE) for lo in range(0, nq, step))


def _attn_p(afq, afk, afv, adq, adkv, c_col, c_t, mb, b, lp):
    nq = lp // LANE
    qblk = lambda c: pl.BlockSpec((LANE, c), lambda bi, i: (bi * nq + i, 0))
    seq = lambda c: pl.BlockSpec((lp, c), lambda bi, i: (bi, 0))
    out = jax.ShapeDtypeStruct((b * lp, FOX_WIDTH), BF16)
    return pl.pallas_call(
        functools.partial(_attn_p_kernel, _attn_buckets(nq)),
        grid=(b, nq),
        in_specs=[qblk(FOX_WIDTH), seq(FOX_WIDTH), seq(FOX_WIDTH), qblk(DSA_WIDTH), seq(2 * DSA_KV_WIDTH),
                  pl.BlockSpec((1, LANE, FOX_HEADS), lambda bi, i: (bi, i, 0)),
                  pl.BlockSpec((1, FOX_HEADS, lp), lambda bi, i: (bi, 0, 0)),
                  pl.BlockSpec((1, LANE, lp), lambda bi, i: (bi, i, 0))],
        out_specs=[qblk(FOX_WIDTH), qblk(DSA_WIDTH)],
        out_shape=[out, out],
        scratch_shapes=[pltpu.VMEM((DSA_GROUP * LANE, lp), BF16)],
        compiler_params=pltpu.CompilerParams(dimension_semantics=("arbitrary", "arbitrary"),
                                             vmem_limit_bytes=VMEM_LIMIT),
        name="attn_p",
    )(afq, afk, afv, adq, adkv, c_col, c_t, mb)


def _select_s_kernel(layer, n_sel, group, pt_ref, cache_ref, iq_ref, misc_ref, out_ref,
                     ik_buf, key_ref, sem):
    n_pages = pt_ref.shape[1]
    past = n_pages * PAGE_SIZE
    t_new = iq_ref.shape[1]
    pw = past + LANE
    step = pl.program_id(0)

    def page_copy(g, p):
        page = pt_ref[step * group + g, p]
        return pltpu.make_async_copy(cache_ref.at[layer, page],
                                     ik_buf.at[g, :, pl.ds(p * PAGE_SIZE, PAGE_SIZE)], sem)

    for g in range(group):
        for p in range(n_pages):
            page_copy(g, p).start()
    for g in range(group):
        for p in range(n_pages):
            page_copy(g, p).wait()

    col = lax.broadcasted_iota(I32, (t_new, pw), 1)
    tok = lax.broadcasted_iota(I32, (t_new, pw), 0)
    visible = (col < past) | (col - past <= tok)
    for g in range(group):
        qf = iq_ref[g]
        q = jnp.concatenate([qf[:, h * LANE:h * LANE + IDX_DIM] for h in range(IDX_HEADS)], axis=0).astype(BF16)
        k_new = jnp.concatenate([misc_ref[g][:, 0:IDX_DIM],
                                 jnp.zeros((LANE - t_new, IDX_DIM), F32)], axis=0).astype(BF16)
        d = jnp.concatenate([_dot(q, ik_buf[g].astype(BF16)), _nt_dot(q, k_new)], axis=1)
        sc = jnp.zeros((t_new, pw), F32)
        for h in range(IDX_HEADS):
            wgt = misc_ref[g][:, M_IW + h:M_IW + h + 1]
            sc = sc + jnp.maximum(d[h * t_new:(h + 1) * t_new], 0.0) * wgt
        key_ref[g * t_new:(g + 1) * t_new, :] = _sort_key(jnp.where(visible, sc, -jnp.inf))

    rows = group * t_new
    nch = pw // LANE
    u_ones, ones = _prefix_consts()
    keys = [key_ref[:, c * LANE:(c + 1) * LANE] for c in range(nch)]

    def bit_step(it, tau):
        cand = tau + jnp.left_shift(jnp.int32(1), 31 - it)
        cnt = _count_rows([k >= cand for k in keys], ones)
        return jnp.where(cnt >= n_sel, cand, tau)

    tau = lax.fori_loop(0, 32, bit_step, jnp.full((rows, LANE), INT_MIN, I32))
    need = n_sel - _count_rows([k > tau for k in keys], ones)
    bias = _selection_bias(keys, tau, need, u_ones, None)
    bias = jnp.where(jnp.concatenate([visible] * group, axis=0), jnp.concatenate(bias, axis=1), NEG)
    for g in range(group):
        out_ref[g] = bias[g * t_new:(g + 1) * t_new]


def _select_s(page_table, cache_ik, iq_s, misc_s, layer, n_sel, group):
    db, t_new, _ = iq_s.shape
    n_pages = page_table.shape[1]
    past = n_pages * PAGE_SIZE
    pw = past + LANE
    blk = lambda c: pl.BlockSpec((group, t_new, c), lambda i, pt: (i, 0, 0))
    return pl.pallas_call(
        functools.partial(_select_s_kernel, layer, n_sel, group),
        grid_spec=pltpu.PrefetchScalarGridSpec(
            num_scalar_prefetch=1, grid=(db // group,),
            in_specs=[pl.BlockSpec(memory_space=pl.ANY), blk(IDX_HEADS * LANE), blk(LANE)],
            out_specs=blk(pw),
            scratch_shapes=[pltpu.VMEM((group, IDX_DIM, past), F32),
                            pltpu.VMEM((group * t_new, pw), I32),
                            pltpu.SemaphoreType.DMA(())]),
        out_shape=jax.ShapeDtypeStruct((db, t_new, pw), F32),
        compiler_params=pltpu.CompilerParams(dimension_semantics=("arbitrary",),
                                             vmem_limit_bytes=VMEM_LIMIT),
        name="select_s",
    )(page_table, cache_ik, iq_s, misc_s)


def _pad_rows(x, n):
    return jnp.concatenate([x, jnp.zeros((n - x.shape[0], x.shape[1]), x.dtype)], axis=0)


def _attn_s_kernel(layer, pt_ref, cfk_ref, cfv_ref, clf_ref, cdk_ref, cdv_ref,
                   fq_ref, dq_ref, fk_ref, fv_ref, dkv_ref, lf_ref, mb_ref, of_ref, od_ref,
                   kbuf, vbuf, lfbuf, dkbuf, dvbuf, sems):
    n_pages = pt_ref.shape[1]
    past = n_pages * PAGE_SIZE
    t_new = fq_ref.shape[1]
    pw = past + LANE
    step = pl.program_id(0)
    nsteps = pl.num_programs(0)
    slot = step % 2

    def copies(seq, sl):
        out = []
        for p in range(n_pages):
            page = pt_ref[seq, p]
            keys = pl.ds(p * PAGE_SIZE, PAGE_SIZE)
            out.append(pltpu.make_async_copy(cfk_ref.at[layer, page], kbuf.at[sl, :, keys], sems.at[sl, 0]))
            out.append(pltpu.make_async_copy(cfv_ref.at[layer, page], vbuf.at[sl, :, keys], sems.at[sl, 1]))
            out.append(pltpu.make_async_copy(clf_ref.at[layer, page], lfbuf.at[sl, :, keys], sems.at[sl, 2]))
            out.append(pltpu.make_async_copy(cdk_ref.at[layer, page], dkbuf.at[sl, :, keys], sems.at[sl, 3]))
            out.append(pltpu.make_async_copy(cdv_ref.at[layer, page], dvbuf.at[sl, :, keys], sems.at[sl, 4]))
        return out

    @pl.when(step == 0)
    def _():
        for c in copies(step, slot):
            c.start()

    @pl.when(step + 1 < nsteps)
    def _():
        for c in copies(step + 1, 1 - slot):
            c.start()

    for c in copies(step, slot):
        c.wait()

    rows = t_new * FOX_HEADS
    r_idx = lax.broadcasted_iota(I32, (rows, pw), 0)
    col = lax.broadcasted_iota(I32, (rows, pw), 1)

    tok = r_idx // FOX_HEADS
    visible = (col < past) | (col - past <= tok)
    lane_w = lax.broadcasted_iota(I32, (rows, FOX_WIDTH), 1)
    row_w = lax.broadcasted_iota(I32, (rows, FOX_WIDTH), 0)
    head_mask = (lane_w // HEAD_DIM) == (row_w % FOX_HEADS)
    q = fq_ref[0]
    q_bd = jnp.where(head_mask, jnp.broadcast_to(q[:, None, :], (t_new, FOX_HEADS, FOX_WIDTH)).reshape(rows, FOX_WIDTH), 0.0)
    q_bd = q_bd.astype(BF16)
    k_past_t = kbuf[slot].astype(BF16)
    v_past_t = vbuf[slot].astype(BF16)
    k_new = _pad_rows(fk_ref[0], LANE).astype(BF16)
    v_new = _pad_rows(fv_ref[0], LANE).astype(BF16)
    s = jnp.concatenate([_dot(q_bd, k_past_t), _nt_dot(q_bd, k_new)], axis=1)

    u = _upper_tri(LANE)
    carry = jnp.zeros((FOX_HEADS, 1), F32)
    cks = []
    for p in range(n_pages):
        out = _cumsum_lanes(lfbuf[slot, :, p * PAGE_SIZE:(p + 1) * PAGE_SIZE], u, carry)
        cks.append(out)
        carry = out[:, LANE - 1:LANE]
    c_new = _cumsum_lanes(lf_ref[0], u, carry)
    cks.append(c_new)
    ck = jnp.concatenate(cks, axis=1)
    ck_rows = jnp.concatenate([ck] * t_new, axis=0)
    c_new_rows = jnp.concatenate([c_new] * t_new, axis=0)
    pick = lax.broadcasted_iota(I32, (rows, LANE), 1) == lax.broadcasted_iota(I32, (rows, LANE), 0) // FOX_HEADS
    cq = jnp.sum(jnp.where(pick, c_new_rows, 0.0), axis=1, keepdims=True)
    s = jnp.where(visible, s + (cq - ck_rows) * LOG2E, NEG)
    o = _softmax_pv(s, v_past_t, v_new)
    o = jnp.where(head_mask, o, 0.0).reshape(t_new, FOX_HEADS, FOX_WIDTH)
    of_ref[0] = jnp.sum(o, axis=1)

    tok_d = r_idx % t_new
    piece = lax.broadcasted_iota(I32, (rows, 1), 0) // t_new
    head = (piece % 2) * DSA_GROUP + piece // 2
    slope = jnp.zeros((rows, 1), F32)
    for h in range(DSA_HEADS):
        slope = jnp.where(head == h, _alibi_slope(h), slope)
    dq = dq_ref[0]
    lane_b = lax.broadcasted_iota(I32, (t_new, LANE), 1)
    pieces = []
    for r in range(DSA_GROUP):
        blk = dq[:, r * LANE:(r + 1) * LANE]
        pieces.append(jnp.where(lane_b < HEAD_DIM, blk, 0.0))
        pieces.append(jnp.where(lane_b < HEAD_DIM, 0.0, blk))
    q_d = jnp.concatenate(pieces, axis=0).astype(BF16)
    dk_past_t = dkbuf[slot].astype(BF16)
    dv_past_t = dvbuf[slot].astype(BF16)
    dkv_new = _pad_rows(dkv_ref[0], LANE).astype(BF16)
    s = jnp.concatenate([_dot(q_d, dk_past_t), _nt_dot(q_d, dkv_new[:, 0:LANE])], axis=1)
    mb = jnp.concatenate([mb_ref[0]] * (rows // t_new), axis=0)
    dist = (past + tok_d - col).astype(F32)
    s = s + (mb - slope * dist)
    o = _softmax_pv(s, dv_past_t, dkv_new[:, LANE:2 * LANE])
    lane_o = lax.broadcasted_iota(I32, (t_new, LANE), 1)
    blocks = []
    for r in range(DSA_GROUP):
        a = o[(2 * r) * t_new:(2 * r + 1) * t_new]
        b = o[(2 * r + 1) * t_new:(2 * r + 2) * t_new]
        blocks.append(jnp.where(lane_o < HEAD_DIM, a, b))
    od_ref[0] = jnp.concatenate(blocks, axis=1)


def _attn_s(page_table, cfk, cfv, clf_t, cdk, cdv, fq_s, dq_s, fk_s, fv_s, dkv_s, lf_s, mb_s, layer):
    db, t_new, _ = fq_s.shape
    n_pages = page_table.shape[1]
    past = n_pages * PAGE_SIZE
    pw = past + LANE
    blk = lambda r, c: pl.BlockSpec((1, r, c), lambda i, pt: (i, 0, 0))
    anyspec = pl.BlockSpec(memory_space=pl.ANY)
    out = jax.ShapeDtypeStruct((db, t_new, FOX_WIDTH), F32)
    return pl.pallas_call(
        functools.partial(_attn_s_kernel, layer),
        grid_spec=pltpu.PrefetchScalarGridSpec(
            num_scalar_prefetch=1, grid=(db,),
            in_specs=[anyspec] * 5 + [blk(t_new, FOX_WIDTH), blk(t_new, DSA_WIDTH), blk(t_new, FOX_WIDTH),
                                      blk(t_new, FOX_WIDTH), blk(t_new, 2 * DSA_KV_WIDTH),
                                      blk(FOX_HEADS, LANE), blk(t_new, pw)],
            out_specs=[blk(t_new, FOX_WIDTH), blk(t_new, DSA_WIDTH)],
            scratch_shapes=[pltpu.VMEM((2, FOX_WIDTH, past), F32),
                            pltpu.VMEM((2, FOX_WIDTH, past), F32),
                            pltpu.VMEM((2, FOX_HEADS, past), F32),
                            pltpu.VMEM((2, DSA_KV_WIDTH, past), F32),
                            pltpu.VMEM((2, DSA_KV_WIDTH, past), F32),
                            pltpu.SemaphoreType.DMA((2, 5))]),
        out_shape=[out, out],
        compiler_params=pltpu.CompilerParams(dimension_semantics=("arbitrary",),
                                             vmem_limit_bytes=VMEM_LIMIT),
        name="attn_s",
    )(page_table, cfk, cfv, clf_t, cdk, cdv, fq_s, dq_s, fk_s, fv_s, dkv_s, lf_s, mb_s)


def _router_gates(s, sb):
    row = lambda a, e: a[e:e + 1, :]
    gscore = []
    for g in range(N_EXPERT_GROUPS):
        a, b, c, d = (row(sb, EXPERTS_PER_GROUP * g + k) for k in range(EXPERTS_PER_GROUP))
        m1, n1 = jnp.maximum(a, b), jnp.minimum(a, b)
        m2, n2 = jnp.maximum(c, d), jnp.minimum(c, d)
        gscore.append(jnp.maximum(m1, m2) + jnp.maximum(jnp.minimum(m1, m2), jnp.maximum(n1, n2)))
    best, gi = gscore[0], jnp.zeros(gscore[0].shape, I32)
    for g in range(1, N_EXPERT_GROUPS):
        better = gscore[g] > best
        gi = jnp.where(better, g, gi)
        best = jnp.where(better, gscore[g], best)

    def in_group(arr, k):
        out = row(arr, k)
        for g in range(1, N_EXPERT_GROUPS):
            out = jnp.where(gi == g, row(arr, EXPERTS_PER_GROUP * g + k), out)
        return out

    vb = [in_group(sb, k) for k in range(EXPERTS_PER_GROUP)]
    vs = [in_group(s, k) for k in range(EXPERTS_PER_GROUP)]

    def arg_first_max(vals):
        best, idx = vals[0], jnp.zeros(vals[0].shape, I32)
        for k in range(1, len(vals)):
            better = vals[k] > best
            idx = jnp.where(better, k, idx)
            best = jnp.where(better, vals[k], best)
        return idx

    i1 = arg_first_max(vb)
    i2 = arg_first_max([jnp.where(i1 == k, -jnp.inf, vb[k]) for k in range(EXPERTS_PER_GROUP)])

    def take(vals, idx):
        out = vals[0]
        for k in range(1, len(vals)):
            out = jnp.where(idx == k, vals[k], out)
        return out

    g1, g2 = take(vs, i1), take(vs, i2)
    den = g1 + g2
    g1, g2 = g1 / den, g2 / den
    e1 = gi * EXPERTS_PER_GROUP + i1
    e2 = gi * EXPERTS_PER_GROUP + i2
    rows = [jnp.where(e1 == e, g1, jnp.where(e2 == e, g2, 0.0)) for e in range(N_EXPERTS)]
    return jnp.concatenate(rows, axis=0)


def _outproj_kernel(alpha, x_ref, mf_ref, md_ref, wf_ref, wd_ref, g_ref, b_ref, wr_ref, br_ref,
                    x1_ref, gates_ref):
    y = _dot(mf_ref[...], wf_ref[...]) + _dot(md_ref[...], wd_ref[...])
    x1 = _layer_norm(alpha * x_ref[...] + y, g_ref[...], b_ref[...])
    x1_ref[...] = x1
    xh, xm, _ = _split3(x1)
    wh, wm, _ = _split3(wr_ref[...])
    logits = _nt_dot(wh, xh) + (_nt_dot(wh, xm) + _nt_dot(wm, xh))
    s = 1.0 / (1.0 + jnp.exp(-logits))
    gates_ref[...] = _router_gates(s, s + br_ref[...])


def _outproj(x, mix_f, mix_d, wo_f, wo_d, g, b, wr_t, br, layer, alpha, tm):
    rt, d = x.shape
    row = lambda c: pl.BlockSpec((tm, c), lambda i: (i, 0))
    lay = lambda r, c: pl.BlockSpec((None, r, c), lambda i: (layer, 0, 0))
    full = lambda r, c: pl.BlockSpec((r, c), lambda i: (0, 0))
    return pl.pallas_call(
        functools.partial(_outproj_kernel, alpha),
        grid=(rt // tm,),
        in_specs=[row(d), row(FOX_WIDTH), row(DSA_WIDTH), lay(FOX_WIDTH, d), lay(DSA_WIDTH, d),
                  lay(1, d), lay(1, d), full(N_EXPERTS, d), full(N_EXPERTS, 1)],
        out_specs=[row(d), pl.BlockSpec((N_EXPERTS, tm), lambda i: (0, i))],
        out_shape=[jax.ShapeDtypeStruct((rt, d), F32), jax.ShapeDtypeStruct((N_EXPERTS, rt), F32)],
        compiler_params=pltpu.CompilerParams(dimension_semantics=("arbitrary",),
                                             vmem_limit_bytes=VMEM_LIMIT),
        name="outproj",
    )(x, mix_f, mix_d, wo_f, wo_d, g, b, wr_t, br)


def _moe_kernel(alpha, x_ref, gate_ref, wg_ref, wu_ref, wd_ref, g_ref, b_ref, out_ref, acc_ref):
    j = pl.program_id(1)

    @pl.when(j == 0)
    def _():
        acc_ref[...] = jnp.zeros(acc_ref.shape, F32)

    xb = x_ref[...].astype(BF16)
    acc = acc_ref[...]
    for e in range(EXPERTS_PER_GROUP):
        hg = _dot(xb, wg_ref[e])
        hu = _dot(xb, wu_ref[e])
        a = (hg * (1.0 / (1.0 + jnp.exp(-hg)))) * hu * gate_ref[:, e:e + 1]
        acc = acc + _dot(a.astype(BF16), wd_ref[e])
    acc_ref[...] = acc

    @pl.when(j == pl.num_programs(1) - 1)
    def _():
        out_ref[...] = _layer_norm(alpha * x_ref[...] + acc_ref[...], g_ref[...], b_ref[...])


def _moe(x1, gates_g, wg, wu, wd, g, b, layer, alpha, tm):
    rt, d = x1.shape
    epg = EXPERTS_PER_GROUP
    return pl.pallas_call(
        functools.partial(_moe_kernel, alpha),
        grid=(rt // tm, N_EXPERT_GROUPS),
        in_specs=[pl.BlockSpec((tm, d), lambda i, j: (i, 0)),
                  pl.BlockSpec((None, tm, epg), lambda i, j: (j, i, 0)),
                  pl.BlockSpec((None, epg, d, D_EXPERT), lambda i, j: (layer, j, 0, 0)),
                  pl.BlockSpec((None, epg, d, D_EXPERT), lambda i, j: (layer, j, 0, 0)),
                  pl.BlockSpec((None, epg, D_EXPERT, d), lambda i, j: (layer, j, 0, 0)),
                  pl.BlockSpec((None, 1, d), lambda i, j: (layer, 0, 0)),
                  pl.BlockSpec((None, 1, d), lambda i, j: (layer, 0, 0))],
        out_specs=pl.BlockSpec((tm, d), lambda i, j: (i, 0)),
        out_shape=jax.ShapeDtypeStruct((rt, d), F32),
        scratch_shapes=[pltpu.VMEM((tm, d), F32)],
        compiler_params=pltpu.CompilerParams(dimension_semantics=("arbitrary", "arbitrary"),
                                             vmem_limit_bytes=VMEM_LIMIT),
        name="moe",
    )(x1, gates_g, wg, wu, wd, g, b)


def _pack_w_in(w_in, b_forget):
    depth, d, _ = w_in.shape
    o = 0
    seg = {}
    for name, size in (("fq", FOX_WIDTH), ("fk", FOX_WIDTH), ("fv", FOX_WIDTH), ("ff", FOX_HEADS),
                       ("dq", DSA_WIDTH), ("dk", DSA_KV_WIDTH), ("dv", DSA_KV_WIDTH),
                       ("iq", IDX_HEADS * IDX_DIM), ("ik", IDX_DIM), ("iw", IDX_HEADS)):
        seg[name] = w_in[:, :, o:o + size]
        o += size
    head_order = [h for r in range(DSA_GROUP) for h in (r, DSA_GROUP + r)]
    dq = seg["dq"].reshape(depth, d, DSA_HEADS, HEAD_DIM)[:, :, head_order].reshape(depth, d, DSA_WIDTH)
    iq = jnp.pad(seg["iq"].reshape(depth, d, IDX_HEADS, IDX_DIM), ((0, 0), (0, 0), (0, 0), (0, LANE - IDX_DIM)))
    iq = iq.reshape(depth, d, IDX_HEADS * LANE)
    misc = jnp.concatenate([seg["ik"], seg["ff"], seg["iw"],
                            jnp.zeros((depth, d, LANE - M_IW - IDX_HEADS), w_in.dtype)], axis=2)
    w = jnp.concatenate([seg["fq"], seg["fk"], seg["fv"], dq, seg["dk"], seg["dv"], iq, misc], axis=2)
    bf = jnp.pad(b_forget.astype(F32), ((0, 0), (M_LOGF, LANE - M_IW)))[:, None, :]
    return w.astype(BF16), bf, head_order


def _row_tile(rt, cap):
    t = cap
    while rt % t:
        t //= 2
    return t


def kernel(x_prompt, x_sample, cache_fox_k, cache_fox_v, cache_fox_logf, cache_dsa_k, cache_dsa_v,
           cache_idx_k, page_table, meta_tokens, w_in, b_forget, w_out, ln1_g, ln1_b, ln2_g, ln2_b,
           w_router, b_router, w_gate, w_up, w_down):
    b, seq, d = x_prompt.shape
    db, t_new, _ = x_sample.shape
    depth = w_in.shape[0]
    n_pool = cache_fox_k.shape[1]
    n_pages = page_table.shape[1]
    past = n_pages * PAGE_SIZE
    l = seq + N_META
    lp = -(-l // LANE) * LANE
    rp = b * lp
    rs = db * t_new
    alpha = (2.0 * depth) ** 0.25
    n_sel_p = min(MAX_SELECT, (l - N_META) // 4)
    n_sel_s = min(MAX_SELECT, (past + t_new) // 4)
    group_s = min(16, db)

    w_packed, bf_packed, head_order = _pack_w_in(w_in, b_forget)
    wo_f = w_out[:, :FOX_WIDTH].astype(BF16)
    wo_d = w_out[:, FOX_WIDTH:].reshape(depth, DSA_HEADS, HEAD_DIM, d)[:, head_order].reshape(depth, DSA_WIDTH, d)
    wo_d = wo_d.astype(BF16)
    wr_t = w_router.T.astype(F32)
    br = b_router.astype(F32)[:, None]
    wg, wu, wd = w_gate.astype(BF16), w_up.astype(BF16), w_down.astype(BF16)
    ln = lambda a: a.astype(F32)[:, None, :]
    g1, b1, g2, b2 = ln(ln1_g), ln(ln1_b), ln(ln2_g), ln(ln2_b)
    keys_minor = lambda c, width: jnp.moveaxis(c, 2, -1).reshape(depth, n_pool, width, PAGE_SIZE)
    cfk = keys_minor(cache_fox_k, FOX_WIDTH)
    cfv = keys_minor(cache_fox_v, FOX_WIDTH)
    clf_t = keys_minor(cache_fox_logf, FOX_HEADS)
    cdk = keys_minor(cache_dsa_k, DSA_KV_WIDTH)
    cdv = keys_minor(cache_dsa_v, DSA_KV_WIDTH)
    cik = keys_minor(cache_idx_k, IDX_DIM)

    meta = jnp.broadcast_to(meta_tokens[None].astype(F32), (b, N_META, d))
    x_p = jnp.concatenate([meta, x_prompt, jnp.zeros((b, lp - l, d), F32)], axis=1).reshape(rp, d)
    x_s = x_sample.reshape(rs, d).astype(F32)

    def new_rows(ofk, ofv, odkv, omisc):
        return (ofk, ofv, omisc[:, M_LOGF:M_IW], odkv[:, :DSA_KV_WIDTH], odkv[:, DSA_KV_WIDTH:], omisc[:, :IDX_DIM])

    def ffn(x, mix_f, mix_d, layer):
        rows = x.shape[0]
        x1, gates_t = _outproj(x, mix_f, mix_d, wo_f, wo_d, g1, b1, wr_t, br, layer, alpha, _row_tile(rows, 512))
        gates_g = jnp.swapaxes(gates_t.reshape(N_EXPERT_GROUPS, EXPERTS_PER_GROUP, rows), 1, 2)
        return _moe(x1, gates_g, wg, wu, wd, g2, b2, layer, alpha, _row_tile(rows, 1024))

    rows_p = [[] for _ in range(6)]
    rows_s = [[] for _ in range(6)]
    for layer in range(depth):
        ofk, ofv, odkv, omisc, afq, afk, afv, adq, adkv, aiq, aik = _proj(
            x_p, w_packed, bf_packed, layer, _row_tile(rp, 512))
        lf_t = jnp.swapaxes(omisc[:, M_LOGF:M_IW].reshape(b, lp, FOX_HEADS), 1, 2)
        c_t = _cumsum(lf_t)
        c_col = jnp.swapaxes(c_t, 1, 2)
        mb_p = _select_p(aiq, aik, omisc, b, lp, n_sel_p)
        mf_p, md_p = _attn_p(afq, afk, afv, adq, adkv, c_col, c_t, mb_p, b, lp)
        for k, a in enumerate(new_rows(ofk, ofv, odkv, omisc)):
            rows_p[k].append(a.reshape(b, lp, -1)[:, :l])

        ofk, ofv, odkv, omisc, afq, afk, afv, adq, adkv, aiq, aik = _proj(
            x_s, w_packed, bf_packed, layer, _row_tile(rs, 512))
        smp = lambda a: a.reshape(db, t_new, a.shape[1]).astype(F32)
        misc_s = smp(omisc)
        mb_s = _select_s(page_table, cik, smp(aiq), misc_s, layer, n_sel_s, group_s)
        lf_s = jnp.pad(jnp.swapaxes(misc_s[:, :, M_LOGF:M_IW], 1, 2), ((0, 0), (0, 0), (0, LANE - t_new)))
        mf_s, md_s = _attn_s(page_table, cfk, cfv, clf_t, cdk, cdv, smp(afq), smp(adq), smp(ofk), smp(ofv),
                             smp(odkv), lf_s, mb_s, layer)
        for k, a in enumerate(new_rows(ofk, ofv, odkv, omisc)):
            rows_s[k].append(a.reshape(db, t_new, -1))

        x_p = ffn(x_p, mf_p, md_p, layer)
        x_s = ffn(x_s, mf_s.reshape(rs, FOX_WIDTH).astype(BF16), md_s.reshape(rs, DSA_WIDTH).astype(BF16), layer)

    y_prompt = x_p.reshape(b, lp, d)[:, N_META:l]
    y_sample = x_s.reshape(db, t_new, d)
    shapes = [(FOX_HEADS, HEAD_DIM), (FOX_HEADS, HEAD_DIM), (FOX_HEADS,), (DSA_KV_HEADS, HEAD_DIM),
              (DSA_KV_HEADS, HEAD_DIM), (IDX_DIM,)]
    outs_p = [jnp.stack(r).reshape((depth, b, l) + s) for r, s in zip(rows_p, shapes)]
    outs_s = [jnp.stack(r).reshape((depth, db, t_new) + s) for r, s in zip(rows_s, shapes)]
    return (y_prompt, y_sample, *outs_p, *outs_s)
```

```python
import functools

import jax
import jax.numpy as jnp
from jax import lax
from jax.experimental import pallas as pl
from jax.experimental.pallas import tpu as pltpu

F32 = jnp.float32
BF16 = jnp.bfloat16
I32 = jnp.int32

HEAD_DIM = 64
FOX_HEADS = 8
DSA_HEADS = 8
DSA_KV_HEADS = 2
DSA_GROUP = DSA_HEADS // DSA_KV_HEADS
IDX_HEADS = 4
IDX_DIM = 64
MAX_SELECT = 256
N_META = 16
PAGE_SIZE = 128
N_EXPERTS = 16
N_EXPERT_GROUPS = 4
EXPERTS_PER_GROUP = N_EXPERTS // N_EXPERT_GROUPS
D_EXPERT = 256
LN_EPS = 1e-5

LANE = 128
FOX_WIDTH = FOX_HEADS * HEAD_DIM
DSA_WIDTH = DSA_HEADS * HEAD_DIM
DSA_KV_WIDTH = DSA_KV_HEADS * HEAD_DIM
C_FQ = 0
C_FK = C_FQ + FOX_WIDTH
C_FV = C_FK + FOX_WIDTH
C_DQ = C_FV + FOX_WIDTH
C_DKV = C_DQ + DSA_WIDTH
C_IQ = C_DKV + 2 * DSA_KV_WIDTH
C_MISC = C_IQ + IDX_HEADS * LANE
N_PACKED = C_MISC + LANE
M_LOGF = IDX_DIM
M_IW = M_LOGF + FOX_HEADS

NEG = -1e30
LOG2E = 1.4426950408889634
Q_SCALE = HEAD_DIM ** -0.5 * LOG2E
INT_MIN = -(2 ** 31)
VMEM_LIMIT = 56 * 1024 * 1024


def _nt_dot(a, b):
    return lax.dot_general(a, b, (((1,), (1,)), ((), ())), preferred_element_type=F32)


def _dot(a, b):
    return jnp.dot(a, b, preferred_element_type=F32)


def _split3(x):
    hi = x.astype(BF16)
    r1 = x - hi.astype(F32)
    mid = r1.astype(BF16)
    lo = (r1 - mid.astype(F32)).astype(BF16)
    return hi, mid, lo


def _upper_tri(n):
    r = lax.broadcasted_iota(I32, (n, n), 0)
    c = lax.broadcasted_iota(I32, (n, n), 1)
    return jnp.where(r <= c, 1.0, 0.0).astype(BF16)


def _layer_norm(z, g, b):
    mu = jnp.mean(z, axis=-1, keepdims=True)
    zc = z - mu
    var = jnp.mean(zc * zc, axis=-1, keepdims=True)
    return zc * lax.rsqrt(var + LN_EPS) * g + b


def _sort_key(x):
    x = jnp.where(x == 0.0, 0.0, x)
    b = lax.bitcast_convert_type(x, I32)
    return b ^ ((b >> 31) & 0x7FFFFFFF)


def _softmax_pv(s, v_t, v_new):
    m = jnp.max(s, axis=1, keepdims=True)
    p = jnp.exp2(s - m)
    l = jnp.sum(p, axis=1, keepdims=True)
    pb = p.astype(BF16)
    w0 = v_t.shape[1]
    return (_nt_dot(pb[:, :w0], v_t) + _dot(pb[:, w0:], v_new)) / l


def _softmax_num(s):
    return jnp.exp2(s - jnp.max(s, axis=1, keepdims=True)).astype(BF16)


def _pv_ones(p, v_ones):
    o = _dot(p, v_ones)
    return o[:, :LANE] / o[:, LANE:]


def _proj_kernel(x_ref, w_ref, bf_ref, ofk, ofv, odkv, omisc, afq, afk, afv, adq, adkv, aiq, aik):
    x = x_ref[...].astype(BF16)

    def seg(a, b):
        return _dot(x, w_ref[:, a:b])

    afq[...] = (seg(C_FQ, C_FK) * Q_SCALE).astype(BF16)
    fk = seg(C_FK, C_FV)
    ofk[...] = fk
    afk[...] = fk.astype(BF16)
    fv = seg(C_FV, C_DQ)
    ofv[...] = fv
    afv[...] = fv.astype(BF16)
    adq[...] = (seg(C_DQ, C_DKV) * Q_SCALE).astype(BF16)
    dkv = seg(C_DKV, C_IQ)
    odkv[...] = dkv
    adkv[...] = dkv.astype(BF16)
    aiq[...] = seg(C_IQ, C_MISC).astype(BF16)
    m = seg(C_MISC, N_PACKED)
    lane = lax.broadcasted_iota(I32, m.shape, 1)
    z = m + bf_ref[...]
    logf = jnp.minimum(z, 0.0) - jnp.log1p(jnp.exp(-jnp.abs(z)))
    iw = m * (IDX_HEADS ** -0.5 * IDX_DIM ** -0.5)
    is_f = (lane >= M_LOGF) & (lane < M_IW)
    is_w = (lane >= M_IW) & (lane < M_IW + IDX_HEADS)
    omisc[...] = jnp.where(is_f, logf, jnp.where(is_w, iw, m))
    aik[...] = m.astype(BF16)


def _proj(x, w_packed, bf_packed, layer, tm):
    rt, d = x.shape
    row = lambda c: pl.BlockSpec((tm, c), lambda i: (i, 0))
    outs = [(FOX_WIDTH, F32), (FOX_WIDTH, F32), (2 * DSA_KV_WIDTH, F32), (LANE, F32),
            (FOX_WIDTH, BF16), (FOX_WIDTH, BF16), (FOX_WIDTH, BF16), (DSA_WIDTH, BF16),
            (2 * DSA_KV_WIDTH, BF16), (IDX_HEADS * LANE, BF16), (LANE, BF16)]
    return pl.pallas_call(
        _proj_kernel,
        grid=(rt // tm,),
        in_specs=[row(d),
                  pl.BlockSpec((None, d, N_PACKED), lambda i: (layer, 0, 0)),
                  pl.BlockSpec((None, 1, LANE), lambda i: (layer, 0, 0))],
        out_specs=[row(c) for c, _ in outs],
        out_shape=[jax.ShapeDtypeStruct((rt, c), dt) for c, dt in outs],
        compiler_params=pltpu.CompilerParams(dimension_semantics=("arbitrary",),
                                             vmem_limit_bytes=VMEM_LIMIT),
        name="proj",
    )(x, w_packed, bf_packed)


def _cumsum_lanes(x, u, carry):
    hi, mid, lo = _split3(x)
    return _dot(hi, u) + _dot(mid, u) + _dot(lo, u) + carry


def _cumsum_kernel(lf_ref, c_ref):
    n = lf_ref.shape[-1] // LANE
    u = _upper_tri(LANE)
    carry = jnp.zeros((lf_ref.shape[1], 1), F32)
    for c in range(n):
        out = _cumsum_lanes(lf_ref[0, :, c * LANE:(c + 1) * LANE], u, carry)
        c_ref[0, :, c * LANE:(c + 1) * LANE] = out * LOG2E
        carry = out[:, LANE - 1:LANE]


def _cumsum(lf_t):
    b, h, lp = lf_t.shape
    spec = pl.BlockSpec((1, h, lp), lambda i: (i, 0, 0))
    return pl.pallas_call(
        _cumsum_kernel, grid=(b,), in_specs=[spec], out_specs=spec,
        out_shape=jax.ShapeDtypeStruct(lf_t.shape, F32),
        compiler_params=pltpu.CompilerParams(dimension_semantics=("arbitrary",)),
        name="cumsum",
    )(lf_t)


def _count_rows(pred_chunks, ones):
    acc = None
    for p in pred_chunks:
        t = jnp.where(p, 1.0, 0.0)
        acc = t if acc is None else acc + t
    return _dot(acc.astype(BF16), ones)


def _selection_bias(keys, tau, need, u_ones, causal_last):
    out = []
    carry = jnp.zeros(tau.shape, F32)
    for c, k in enumerate(keys):
        eq = k == tau
        eqf = jnp.where(eq, 1.0, 0.0)
        pr = _dot(eqf.astype(BF16), u_ones)
        rank = carry + pr[:, :LANE] - eqf
        b = jnp.where(k > tau, 0.0, jnp.where(eq, jnp.where(rank < need, 0.0, NEG), NEG))
        if c == len(keys) - 1 and causal_last is not None:
            b = jnp.where(causal_last, b, NEG)
        out.append(b)
        carry = carry + pr[:, LANE:]
    return out


def _prefix_consts():
    u = _upper_tri(LANE)
    ones = jnp.ones((LANE, LANE), BF16)
    return jnp.concatenate([u, ones], axis=1), ones


def _select_p_kernel(n_sel, iq_ref, ik_ref, misc_ref, out_ref, key_ref, tau_ref):
    lp = iq_ref.shape[0]
    nq = lp // LANE
    offs = [i * (i + 1) // 2 for i in range(nq)]
    u_ones, ones = _prefix_consts()
    rloc = lax.broadcasted_iota(I32, (LANE, LANE), 0)
    cloc = lax.broadcasted_iota(I32, (LANE, LANE), 1)
    diag = cloc <= rloc

    for i in range(nq):
        w = (i + 1) * LANE
        rows = slice(i * LANE, (i + 1) * LANE)
        q = jnp.concatenate([iq_ref[rows, h * LANE:h * LANE + IDX_DIM] for h in range(IDX_HEADS)], axis=0)
        d = _nt_dot(q, ik_ref[0:w, 0:IDX_DIM])
        sc = jnp.zeros((LANE, w), F32)
        for h in range(IDX_HEADS):
            wgt = misc_ref[rows, M_IW + h:M_IW + h + 1]
            sc = sc + jnp.maximum(d[h * LANE:(h + 1) * LANE], 0.0) * wgt
        for c in range(i + 1):
            s_c = sc[:, c * LANE:(c + 1) * LANE]
            if c == i:
                s_c = jnp.where(diag, s_c, -jnp.inf)
            key_ref[offs[i] + c] = _sort_key(s_c)
        tau_ref[i] = jnp.full((LANE, LANE), INT_MIN, I32)

    def bit_step(it, carry):
        bit = jnp.left_shift(jnp.int32(1), 31 - it)
        for i in range(nq):
            cand = tau_ref[i] + bit
            cnt = _count_rows([key_ref[offs[i] + c] >= cand for c in range(i + 1)], ones)
            tau_ref[i] = jnp.where(cnt >= n_sel, cand, tau_ref[i])
        return carry

    lax.fori_loop(0, 32, bit_step, 0)

    neg_blk = jnp.full((LANE, LANE), NEG, BF16)
    for i in range(nq):
        tau = tau_ref[i]
        keys = [key_ref[offs[i] + c] for c in range(i + 1)]
        need = n_sel - _count_rows([k > tau for k in keys], ones)
        bias = _selection_bias(keys, tau, need, u_ones, diag)
        for c in range(nq):
            blk = bias[c].astype(BF16) if c <= i else neg_blk
            out_ref[0, i * LANE:(i + 1) * LANE, c * LANE:(c + 1) * LANE] = blk


def _select_p(aiq, aik, omisc, b, lp, n_sel):
    nq = lp // LANE
    return pl.pallas_call(
        functools.partial(_select_p_kernel, n_sel),
        grid=(b,),
        in_specs=[pl.BlockSpec((lp, IDX_HEADS * LANE), lambda i: (i, 0)),
                  pl.BlockSpec((lp, LANE), lambda i: (i, 0)),
                  pl.BlockSpec((lp, LANE), lambda i: (i, 0))],
        out_specs=pl.BlockSpec((1, lp, lp), lambda i: (i, 0, 0)),
        out_shape=jax.ShapeDtypeStruct((b, lp, lp), BF16),
        scratch_shapes=[pltpu.VMEM((nq * (nq + 1) // 2, LANE, LANE), I32),
                        pltpu.VMEM((nq, LANE, LANE), I32)],
        compiler_params=pltpu.CompilerParams(dimension_semantics=("arbitrary",),
                                             vmem_limit_bytes=VMEM_LIMIT),
        name="select_p",
    )(aiq, aik, omisc)


def _alibi_slope(head):
    return 2.0 ** (-(8.0 / DSA_HEADS) * (head + 1)) * LOG2E


def _attn_p_body(w, i, fq_ref, fk_ref, fv_ref, dq_ref, dkv_ref, cq_ref, ck_ref, mb_ref, of_ref, od_ref,
                 p_ref):
    tq = fq_ref.shape[0]
    lane = lax.broadcasted_iota(I32, (tq, LANE), 1)
    lo_half = lane < HEAD_DIM
    qpos = i * tq + lax.broadcasted_iota(I32, (tq, w), 0)
    kpos = lax.broadcasted_iota(I32, (tq, w), 1)
    causal = kpos <= qpos
    ones = jnp.ones((w, LANE), BF16)

    for j in range(FOX_HEADS // 2):
        cols = slice(j * LANE, (j + 1) * LANE)
        qp = fq_ref[:, cols]
        q2 = jnp.concatenate([jnp.where(lo_half, qp, 0), jnp.where(lo_half, 0, qp)], axis=0)
        s = _nt_dot(q2, fk_ref[0:w, cols])
        for hh in range(2):
            h = 2 * j + hh
            t = s[hh * tq:(hh + 1) * tq] + (cq_ref[0, :, h:h + 1] - ck_ref[0, h:h + 1, 0:w])
            p_ref[hh * tq:(hh + 1) * tq, 0:w] = _softmax_num(jnp.where(causal, t, NEG))
        o = _pv_ones(p_ref[0:2 * tq, 0:w], jnp.concatenate([fv_ref[0:w, cols], ones], axis=1))
        of_ref[:, cols] = jnp.where(lo_half, o[:tq], o[tq:]).astype(BF16)

    dist = (qpos - kpos).astype(F32)
    mb = mb_ref[0, :, 0:w].astype(F32)
    v_ones = jnp.concatenate([dkv_ref[0:w, LANE:2 * LANE], ones], axis=1)
    outs = []
    for g in range(DSA_KV_HEADS):
        keep = lo_half if g == 0 else jnp.logical_not(lo_half)
        q4 = jnp.concatenate(
            [jnp.where(keep, dq_ref[:, r * LANE:(r + 1) * LANE], 0) for r in range(DSA_GROUP)], axis=0)
        s = _nt_dot(q4, dkv_ref[0:w, 0:LANE])
        for r in range(DSA_GROUP):
            t = s[r * tq:(r + 1) * tq] + (mb - _alibi_slope(g * DSA_GROUP + r) * dist)
            p_ref[r * tq:(r + 1) * tq, 0:w] = _softmax_num(t)
        outs.append(_pv_ones(p_ref[0:DSA_GROUP * tq, 0:w], v_ones))
    for r in range(DSA_GROUP):
        rows = slice(r * tq, (r + 1) * tq)
        od_ref[:, r * LANE:(r + 1) * LANE] = jnp.where(lo_half, outs[0][rows], outs[1][rows]).astype(BF16)


def _attn_p_kernel(buckets, *refs):
    i = pl.program_id(1)
    for lo, hi, w in buckets:
        @pl.when((i >= lo) & (i < hi))
        def _():
            _attn_p_body(w, i, *refs)


def _attn_buckets(nq):
    step = 4
    return tuple((lo, min(lo + step, nq), min(lo + step, nq) * LANE) for lo in range(0, nq, step))


def _attn_p(afq, afk, afv, adq, adkv, c_col, c_t, mb, b, lp):
    nq = lp // LANE
    qblk = lambda c: pl.BlockSpec((LANE, c), lambda bi, i: (bi * nq + i, 0))
    seq = lambda c: pl.BlockSpec((lp, c), lambda bi, i: (bi, 0))
    out = jax.ShapeDtypeStruct((b * lp, FOX_WIDTH), BF16)
    return pl.pallas_call(
        functools.partial(_attn_p_kernel, _attn_buckets(nq)),
        grid=(b, nq),
        in_specs=[qblk(FOX_WIDTH), seq(FOX_WIDTH), seq(FOX_WIDTH), qblk(DSA_WIDTH), seq(2 * DSA_KV_WIDTH),
                  pl.BlockSpec((1, LANE, FOX_HEADS), lambda bi, i: (bi, i, 0)),
                  pl.BlockSpec((1, FOX_HEADS, lp), lambda bi, i: (bi, 0, 0)),
                  pl.BlockSpec((1, LANE, lp), lambda bi, i: (bi, i, 0))],
        out_specs=[qblk(FOX_WIDTH), qblk(DSA_WIDTH)],
        out_shape=[out, out],
        scratch_shapes=[pltpu.VMEM((DSA_GROUP * LANE, lp), BF16)],
        compiler_params=pltpu.CompilerParams(dimension_semantics=("arbitrary", "arbitrary"),
                                             vmem_limit_bytes=VMEM_LIMIT),
        name="attn_p",
    )(afq, afk, afv, adq, adkv, c_col, c_t, mb)


def _select_s_kernel(layer, n_sel, group, pt_ref, cache_ref, iq_ref, misc_ref, out_ref,
                     ik_buf, key_ref, sem):
    n_pages = pt_ref.shape[1]
    past = n_pages * PAGE_SIZE
    t_new = iq_ref.shape[1]
    pw = past + LANE
    step = pl.program_id(0)

    def page_copy(g, p):
        page = pt_ref[step * group + g, p]
        return pltpu.make_async_copy(cache_ref.at[layer, page],
                                     ik_buf.at[g, :, pl.ds(p * PAGE_SIZE, PAGE_SIZE)], sem)

    for g in range(group):
        for p in range(n_pages):
            page_copy(g, p).start()
    for g in range(group):
        for p in range(n_pages):
            page_copy(g, p).wait()

    col = lax.broadcasted_iota(I32, (t_new, pw), 1)
    tok = lax.broadcasted_iota(I32, (t_new, pw), 0)
    visible = (col < past) | (col - past <= tok)
    for g in range(group):
        qf = iq_ref[g]
        q = jnp.concatenate([qf[:, h * LANE:h * LANE + IDX_DIM] for h in range(IDX_HEADS)], axis=0).astype(BF16)
        k_new = jnp.concatenate([misc_ref[g][:, 0:IDX_DIM],
                                 jnp.zeros((LANE - t_new, IDX_DIM), F32)], axis=0).astype(BF16)
        d = jnp.concatenate([_dot(q, ik_buf[g].astype(BF16)), _nt_dot(q, k_new)], axis=1)
        sc = jnp.zeros((t_new, pw), F32)
        for h in range(IDX_HEADS):
            wgt = misc_ref[g][:, M_IW + h:M_IW + h + 1]
            sc = sc + jnp.maximum(d[h * t_new:(h + 1) * t_new], 0.0) * wgt
        key_ref[g * t_new:(g + 1) * t_new, :] = _sort_key(jnp.where(visible, sc, -jnp.inf))

    rows = group * t_new
    nch = pw // LANE
    u_ones, ones = _prefix_consts()
    keys = [key_ref[:, c * LANE:(c + 1) * LANE] for c in range(nch)]

    def bit_step(it, tau):
        cand = tau + jnp.left_shift(jnp.int32(1), 31 - it)
        cnt = _count_rows([k >= cand for k in keys], ones)
        return jnp.where(cnt >= n_sel, cand, tau)

    tau = lax.fori_loop(0, 32, bit_step, jnp.full((rows, LANE), INT_MIN, I32))
    need = n_sel - _count_rows([k > tau for k in keys], ones)
    bias = _selection_bias(keys, tau, need, u_ones, None)
    bias = jnp.where(jnp.concatenate([visible] * group, axis=0), jnp.concatenate(bias, axis=1), NEG)
    for g in range(group):
        out_ref[g] = bias[g * t_new:(g + 1) * t_new]


def _select_s(page_table, cache_ik, iq_s, misc_s, layer, n_sel, group):
    db, t_new, _ = iq_s.shape
    n_pages = page_table.shape[1]
    past = n_pages * PAGE_SIZE
    pw = past + LANE
    blk = lambda c: pl.BlockSpec((group, t_new, c), lambda i, pt: (i, 0, 0))
    return pl.pallas_call(
        functools.partial(_select_s_kernel, layer, n_sel, group),
        grid_spec=pltpu.PrefetchScalarGridSpec(
            num_scalar_prefetch=1, grid=(db // group,),
            in_specs=[pl.BlockSpec(memory_space=pl.ANY), blk(IDX_HEADS * LANE), blk(LANE)],
            out_specs=blk(pw),
            scratch_shapes=[pltpu.VMEM((group, IDX_DIM, past), F32),
                            pltpu.VMEM((group * t_new, pw), I32),
                            pltpu.SemaphoreType.DMA(())]),
        out_shape=jax.ShapeDtypeStruct((db, t_new, pw), F32),
        compiler_params=pltpu.CompilerParams(dimension_semantics=("arbitrary",),
                                             vmem_limit_bytes=VMEM_LIMIT),
        name="select_s",
    )(page_table, cache_ik, iq_s, misc_s)


def _pad_rows(x, n):
    return jnp.concatenate([x, jnp.zeros((n - x.shape[0], x.shape[1]), x.dtype)], axis=0)


def _attn_s_kernel(layer, pt_ref, cfk_ref, cfv_ref, clf_ref, cdk_ref, cdv_ref,
                   fq_ref, dq_ref, fk_ref, fv_ref, dkv_ref, lf_ref, mb_ref, of_ref, od_ref,
                   kbuf, vbuf, lfbuf, dkbuf, dvbuf, sems):
    n_pages = pt_ref.shape[1]
    past = n_pages * PAGE_SIZE
    t_new = fq_ref.shape[1]
    pw = past + LANE
    step = pl.program_id(0)
    nsteps = pl.num_programs(0)
    slot = step % 2

    def copies(seq, sl):
        out = []
        for p in range(n_pages):
            page = pt_ref[seq, p]
            keys = pl.ds(p * PAGE_SIZE, PAGE_SIZE)
            out.append(pltpu.make_async_copy(cfk_ref.at[layer, page], kbuf.at[sl, :, keys], sems.at[sl, 0]))
            out.append(pltpu.make_async_copy(cfv_ref.at[layer, page], vbuf.at[sl, :, keys], sems.at[sl, 1]))
            out.append(pltpu.make_async_copy(clf_ref.at[layer, page], lfbuf.at[sl, :, keys], sems.at[sl, 2]))
            out.append(pltpu.make_async_copy(cdk_ref.at[layer, page], dkbuf.at[sl, :, keys], sems.at[sl, 3]))
            out.append(pltpu.make_async_copy(cdv_ref.at[layer, page], dvbuf.at[sl, :, keys], sems.at[sl, 4]))
        return out

    @pl.when(step == 0)
    def _():
        for c in copies(step, slot):
            c.start()

    @pl.when(step + 1 < nsteps)
    def _():
        for c in copies(step + 1, 1 - slot):
            c.start()

    for c in copies(step, slot):
        c.wait()

    rows = t_new * FOX_HEADS
    r_idx = lax.broadcasted_iota(I32, (rows, pw), 0)
    col = lax.broadcasted_iota(I32, (rows, pw), 1)

    tok = r_idx // FOX_HEADS
    visible = (col < past) | (col - past <= tok)
    lane_w = lax.broadcasted_iota(I32, (rows, FOX_WIDTH), 1)
    row_w = lax.broadcasted_iota(I32, (rows, FOX_WIDTH), 0)
    head_mask = (lane_w // HEAD_DIM) == (row_w % FOX_HEADS)
    q = fq_ref[0]
    q_bd = jnp.where(head_mask, jnp.broadcast_to(q[:, None, :], (t_new, FOX_HEADS, FOX_WIDTH)).reshape(rows, FOX_WIDTH), 0.0)
    q_bd = q_bd.astype(BF16)
    k_past_t = kbuf[slot].astype(BF16)
    v_past_t = vbuf[slot].astype(BF16)
    k_new = _pad_rows(fk_ref[0], LANE).astype(BF16)
    v_new = _pad_rows(fv_ref[0], LANE).astype(BF16)
    s = jnp.concatenate([_dot(q_bd, k_past_t), _nt_dot(q_bd, k_new)], axis=1)

    u = _upper_tri(LANE)
    carry = jnp.zeros((FOX_HEADS, 1), F32)
    cks = []
    for p in range(n_pages):
        out = _cumsum_lanes(lfbuf[slot, :, p * PAGE_SIZE:(p + 1) * PAGE_SIZE], u, carry)
        cks.append(out)
        carry = out[:, LANE - 1:LANE]
    c_new = _cumsum_lanes(lf_ref[0], u, carry)
    cks.append(c_new)
    ck = jnp.concatenate(cks, axis=1)
    ck_rows = jnp.concatenate([ck] * t_new, axis=0)
    c_new_rows = jnp.concatenate([c_new] * t_new, axis=0)
    pick = lax.broadcasted_iota(I32, (rows, LANE), 1) == lax.broadcasted_iota(I32, (rows, LANE), 0) // FOX_HEADS
    cq = jnp.sum(jnp.where(pick, c_new_rows, 0.0), axis=1, keepdims=True)
    s = jnp.where(visible, s + (cq - ck_rows) * LOG2E, NEG)
    o = _softmax_pv(s, v_past_t, v_new)
    o = jnp.where(head_mask, o, 0.0).reshape(t_new, FOX_HEADS, FOX_WIDTH)
    of_ref[0] = jnp.sum(o, axis=1)

    tok_d = r_idx % t_new
    piece = lax.broadcasted_iota(I32, (rows, 1), 0) // t_new
    head = (piece % 2) * DSA_GROUP + piece // 2
    slope = jnp.zeros((rows, 1), F32)
    for h in range(DSA_HEADS):
        slope = jnp.where(head == h, _alibi_slope(h), slope)
    dq = dq_ref[0]
    lane_b = lax.broadcasted_iota(I32, (t_new, LANE), 1)
    pieces = []
    for r in range(DSA_GROUP):
        blk = dq[:, r * LANE:(r + 1) * LANE]
        pieces.append(jnp.where(lane_b < HEAD_DIM, blk, 0.0))
        pieces.append(jnp.where(lane_b < HEAD_DIM, 0.0, blk))
    q_d = jnp.concatenate(pieces, axis=0).astype(BF16)
    dk_past_t = dkbuf[slot].astype(BF16)
    dv_past_t = dvbuf[slot].astype(BF16)
    dkv_new = _pad_rows(dkv_ref[0], LANE).astype(BF16)
    s = jnp.concatenate([_dot(q_d, dk_past_t), _nt_dot(q_d, dkv_new[:, 0:LANE])], axis=1)
    mb = jnp.concatenate([mb_ref[0]] * (rows // t_new), axis=0)
    dist = (past + tok_d - col).astype(F32)
    s = s + (mb - slope * dist)
    o = _softmax_pv(s, dv_past_t, dkv_new[:, LANE:2 * LANE])
    lane_o = lax.broadcasted_iota(I32, (t_new, LANE), 1)
    blocks = []
    for r in range(DSA_GROUP):
        a = o[(2 * r) * t_new:(2 * r + 1) * t_new]
        b = o[(2 * r + 1) * t_new:(2 * r + 2) * t_new]
        blocks.append(jnp.where(lane_o < HEAD_DIM, a, b))
    od_ref[0] = jnp.concatenate(blocks, axis=1)


def _attn_s(page_table, cfk, cfv, clf_t, cdk, cdv, fq_s, dq_s, fk_s, fv_s, dkv_s, lf_s, mb_s, layer):
    db, t_new, _ = fq_s.shape
    n_pages = page_table.shape[1]
    past = n_pages * PAGE_SIZE
    pw = past + LANE
    blk = lambda r, c: pl.BlockSpec((1, r, c), lambda i, pt: (i, 0, 0))
    anyspec = pl.BlockSpec(memory_space=pl.ANY)
    out = jax.ShapeDtypeStruct((db, t_new, FOX_WIDTH), F32)
    return pl.pallas_call(
        functools.partial(_attn_s_kernel, layer),
        grid_spec=pltpu.PrefetchScalarGridSpec(
            num_scalar_prefetch=1, grid=(db,),
            in_specs=[anyspec] * 5 + [blk(t_new, FOX_WIDTH), blk(t_new, DSA_WIDTH), blk(t_new, FOX_WIDTH),
                                      blk(t_new, FOX_WIDTH), blk(t_new, 2 * DSA_KV_WIDTH),
                                      blk(FOX_HEADS, LANE), blk(t_new, pw)],
            out_specs=[blk(t_new, FOX_WIDTH), blk(t_new, DSA_WIDTH)],
            scratch_shapes=[pltpu.VMEM((2, FOX_WIDTH, past), F32),
                            pltpu.VMEM((2, FOX_WIDTH, past), F32),
                            pltpu.VMEM((2, FOX_HEADS, past), F32),
                            pltpu.VMEM((2, DSA_KV_WIDTH, past), F32),
                            pltpu.VMEM((2, DSA_KV_WIDTH, past), F32),
                            pltpu.SemaphoreType.DMA((2, 5))]),
        out_shape=[out, out],
        compiler_params=pltpu.CompilerParams(dimension_semantics=("arbitrary",),
                                             vmem_limit_bytes=VMEM_LIMIT),
        name="attn_s",
    )(page_table, cfk, cfv, clf_t, cdk, cdv, fq_s, dq_s, fk_s, fv_s, dkv_s, lf_s, mb_s)


def _router_gates(s, sb):
    row = lambda a, e: a[e:e + 1, :]
    gscore = []
    for g in range(N_EXPERT_GROUPS):
        a, b, c, d = (row(sb, EXPERTS_PER_GROUP * g + k) for k in range(EXPERTS_PER_GROUP))
        m1, n1 = jnp.maximum(a, b), jnp.minimum(a, b)
        m2, n2 = jnp.maximum(c, d), jnp.minimum(c, d)
        gscore.append(jnp.maximum(m1, m2) + jnp.maximum(jnp.minimum(m1, m2), jnp.maximum(n1, n2)))
    best, gi = gscore[0], jnp.zeros(gscore[0].shape, I32)
    for g in range(1, N_EXPERT_GROUPS):
        better = gscore[g] > best
        gi = jnp.where(better, g, gi)
        best = jnp.where(better, gscore[g], best)

    def in_group(arr, k):
        out = row(arr, k)
        for g in range(1, N_EXPERT_GROUPS):
            out = jnp.where(gi == g, row(arr, EXPERTS_PER_GROUP * g + k), out)
        return out

    vb = [in_group(sb, k) for k in range(EXPERTS_PER_GROUP)]
    vs = [in_group(s, k) for k in range(EXPERTS_PER_GROUP)]

    def arg_first_max(vals):
        best, idx = vals[0], jnp.zeros(vals[0].shape, I32)
        for k in range(1, len(vals)):
            better = vals[k] > best
            idx = jnp.where(better, k, idx)
            best = jnp.where(better, vals[k], best)
        return idx

    i1 = arg_first_max(vb)
    i2 = arg_first_max([jnp.where(i1 == k, -jnp.inf, vb[k]) for k in range(EXPERTS_PER_GROUP)])

    def take(vals, idx):
        out = vals[0]
        for k in range(1, len(vals)):
            out = jnp.where(idx == k, vals[k], out)
        return out

    g1, g2 = take(vs, i1), take(vs, i2)
    den = g1 + g2
    g1, g2 = g1 / den, g2 / den
    e1 = gi * EXPERTS_PER_GROUP + i1
    e2 = gi * EXPERTS_PER_GROUP + i2
    rows = [jnp.where(e1 == e, g1, jnp.where(e2 == e, g2, 0.0)) for e in range(N_EXPERTS)]
    return jnp.concatenate(rows, axis=0)


def _outproj_kernel(alpha, x_ref, mf_ref, md_ref, wf_ref, wd_ref, g_ref, b_ref, wr_ref, br_ref,
                    x1_ref, gates_ref):
    y = _dot(mf_ref[...], wf_ref[...]) + _dot(md_ref[...], wd_ref[...])
    x1 = _layer_norm(alpha * x_ref[...] + y, g_ref[...], b_ref[...])
    x1_ref[...] = x1
    xh, xm, _ = _split3(x1)
    wh, wm, _ = _split3(wr_ref[...])
    logits = _nt_dot(wh, xh) + (_nt_dot(wh, xm) + _nt_dot(wm, xh))
    s = 1.0 / (1.0 + jnp.exp(-logits))
    gates_ref[...] = _router_gates(s, s + br_ref[...])


def _outproj(x, mix_f, mix_d, wo_f, wo_d, g, b, wr_t, br, layer, alpha, tm):
    rt, d = x.shape
    row = lambda c: pl.BlockSpec((tm, c), lambda i: (i, 0))
    lay = lambda r, c: pl.BlockSpec((None, r, c), lambda i: (layer, 0, 0))
    full = lambda r, c: pl.BlockSpec((r, c), lambda i: (0, 0))
    return pl.pallas_call(
        functools.partial(_outproj_kernel, alpha),
        grid=(rt // tm,),
        in_specs=[row(d), row(FOX_WIDTH), row(DSA_WIDTH), lay(FOX_WIDTH, d), lay(DSA_WIDTH, d),
                  lay(1, d), lay(1, d), full(N_EXPERTS, d), full(N_EXPERTS, 1)],
        out_specs=[row(d), pl.BlockSpec((N_EXPERTS, tm), lambda i: (0, i))],
        out_shape=[jax.ShapeDtypeStruct((rt, d), F32), jax.ShapeDtypeStruct((N_EXPERTS, rt), F32)],
        compiler_params=pltpu.CompilerParams(dimension_semantics=("arbitrary",),
                                             vmem_limit_bytes=VMEM_LIMIT),
        name="outproj",
    )(x, mix_f, mix_d, wo_f, wo_d, g, b, wr_t, br)


def _moe_kernel(alpha, x_ref, gate_ref, wg_ref, wu_ref, wd_ref, g_ref, b_ref, out_ref, acc_ref):
    j = pl.program_id(1)

    @pl.when(j == 0)
    def _():
        acc_ref[...] = jnp.zeros(acc_ref.shape, F32)

    xb = x_ref[...].astype(BF16)
    acc = acc_ref[...]
    for e in range(EXPERTS_PER_GROUP):
        hg = _dot(xb, wg_ref[e])
        hu = _dot(xb, wu_ref[e])
        a = (hg * (1.0 / (1.0 + jnp.exp(-hg)))) * hu * gate_ref[:, e:e + 1]
        acc = acc + _dot(a.astype(BF16), wd_ref[e])
    acc_ref[...] = acc

    @pl.when(j == pl.num_programs(1) - 1)
    def _():
        out_ref[...] = _layer_norm(alpha * x_ref[...] + acc_ref[...], g_ref[...], b_ref[...])


def _moe(x1, gates_g, wg, wu, wd, g, b, layer, alpha, tm):
    rt, d = x1.shape
    epg = EXPERTS_PER_GROUP
    return pl.pallas_call(
        functools.partial(_moe_kernel, alpha),
        grid=(rt // tm, N_EXPERT_GROUPS),
        in_specs=[pl.BlockSpec((tm, d), lambda i, j: (i, 0)),
                  pl.BlockSpec((None, tm, epg), lambda i, j: (j, i, 0)),
                  pl.BlockSpec((None, epg, d, D_EXPERT), lambda i, j: (layer, j, 0, 0)),
                  pl.BlockSpec((None, epg, d, D_EXPERT), lambda i, j: (layer, j, 0, 0)),
                  pl.BlockSpec((None, epg, D_EXPERT, d), lambda i, j: (layer, j, 0, 0)),
                  pl.BlockSpec((None, 1, d), lambda i, j: (layer, 0, 0)),
                  pl.BlockSpec((None, 1, d), lambda i, j: (layer, 0, 0))],
        out_specs=pl.BlockSpec((tm, d), lambda i, j: (i, 0)),
        out_shape=jax.ShapeDtypeStruct((rt, d), F32),
        scratch_shapes=[pltpu.VMEM((tm, d), F32)],
        compiler_params=pltpu.CompilerParams(dimension_semantics=("arbitrary", "arbitrary"),
                                             vmem_limit_bytes=VMEM_LIMIT),
        name="moe",
    )(x1, gates_g, wg, wu, wd, g, b)


def _pack_w_in(w_in, b_forget):
    depth, d, _ = w_in.shape
    o = 0
    seg = {}
    for name, size in (("fq", FOX_WIDTH), ("fk", FOX_WIDTH), ("fv", FOX_WIDTH), ("ff", FOX_HEADS),
                       ("dq", DSA_WIDTH), ("dk", DSA_KV_WIDTH), ("dv", DSA_KV_WIDTH),
                       ("iq", IDX_HEADS * IDX_DIM), ("ik", IDX_DIM), ("iw", IDX_HEADS)):
        seg[name] = w_in[:, :, o:o + size]
        o += size
    head_order = [h for r in range(DSA_GROUP) for h in (r, DSA_GROUP + r)]
    dq = seg["dq"].reshape(depth, d, DSA_HEADS, HEAD_DIM)[:, :, head_order].reshape(depth, d, DSA_WIDTH)
    iq = jnp.pad(seg["iq"].reshape(depth, d, IDX_HEADS, IDX_DIM), ((0, 0), (0, 0), (0, 0), (0, LANE - IDX_DIM)))
    iq = iq.reshape(depth, d, IDX_HEADS * LANE)
    misc = jnp.concatenate([seg["ik"], seg["ff"], seg["iw"],
                            jnp.zeros((depth, d, LANE - M_IW - IDX_HEADS), w_in.dtype)], axis=2)
    w = jnp.concatenate([seg["fq"], seg["fk"], seg["fv"], dq, seg["dk"], seg["dv"], iq, misc], axis=2)
    bf = jnp.pad(b_forget.astype(F32), ((0, 0), (M_LOGF, LANE - M_IW)))[:, None, :]
    return w.astype(BF16), bf, head_order


def _row_tile(rt, cap):
    t = cap
    while rt % t:
        t //= 2
    return t


def kernel(x_prompt, x_sample, cache_fox_k, cache_fox_v, cache_fox_logf, cache_dsa_k, cache_dsa_v,
           cache_idx_k, page_table, meta_tokens, w_in, b_forget, w_out, ln1_g, ln1_b, ln2_g, ln2_b,
           w_router, b_router, w_gate, w_up, w_down):
    b, seq, d = x_prompt.shape
    db, t_new, _ = x_sample.shape
    depth = w_in.shape[0]
    n_pool = cache_fox_k.shape[1]
    n_pages = page_table.shape[1]
    past = n_pages * PAGE_SIZE
    l = seq + N_META
    lp = -(-l // LANE) * LANE
    rp = b * lp
    rs = db * t_new
    alpha = (2.0 * depth) ** 0.25
    n_sel_p = min(MAX_SELECT, (l - N_META) // 4)
    n_sel_s = min(MAX_SELECT, (past + t_new) // 4)
    group_s = min(16, db)

    w_packed, bf_packed, head_order = _pack_w_in(w_in, b_forget)
    wo_f = w_out[:, :FOX_WIDTH].astype(BF16)
    wo_d = w_out[:, FOX_WIDTH:].reshape(depth, DSA_HEADS, HEAD_DIM, d)[:, head_order].reshape(depth, DSA_WIDTH, d)
    wo_d = wo_d.astype(BF16)
    wr_t = w_router.T.astype(F32)
    br = b_router.astype(F32)[:, None]
    wg, wu, wd = w_gate.astype(BF16), w_up.astype(BF16), w_down.astype(BF16)
    ln = lambda a: a.astype(F32)[:, None, :]
    g1, b1, g2, b2 = ln(ln1_g), ln(ln1_b), ln(ln2_g), ln(ln2_b)
    keys_minor = lambda c, width: jnp.moveaxis(c, 2, -1).reshape(depth, n_pool, width, PAGE_SIZE)
    cfk = keys_minor(cache_fox_k, FOX_WIDTH)
    cfv = keys_minor(cache_fox_v, FOX_WIDTH)
    clf_t = keys_minor(cache_fox_logf, FOX_HEADS)
    cdk = keys_minor(cache_dsa_k, DSA_KV_WIDTH)
    cdv = keys_minor(cache_dsa_v, DSA_KV_WIDTH)
    cik = keys_minor(cache_idx_k, IDX_DIM)

    meta = jnp.broadcast_to(meta_tokens[None].astype(F32), (b, N_META, d))
    x_p = jnp.concatenate([meta, x_prompt, jnp.zeros((b, lp - l, d), F32)], axis=1).reshape(rp, d)
    x_s = x_sample.reshape(rs, d).astype(F32)

    def new_rows(ofk, ofv, odkv, omisc):
        return (ofk, ofv, omisc[:, M_LOGF:M_IW], odkv[:, :DSA_KV_WIDTH], odkv[:, DSA_KV_WIDTH:], omisc[:, :IDX_DIM])

    def ffn(x, mix_f, mix_d, layer):
        rows = x.shape[0]
        x1, gates_t = _outproj(x, mix_f, mix_d, wo_f, wo_d, g1, b1, wr_t, br, layer, alpha, _row_tile(rows, 512))
        gates_g = jnp.swapaxes(gates_t.reshape(N_EXPERT_GROUPS, EXPERTS_PER_GROUP, rows), 1, 2)
        return _moe(x1, gates_g, wg, wu, wd, g2, b2, layer, alpha, _row_tile(rows, 1024))

    rows_p = [[] for _ in range(6)]
    rows_s = [[] for _ in range(6)]
    for layer in range(depth):
        ofk, ofv, odkv, omisc, afq, afk, afv, adq, adkv, aiq, aik = _proj(
            x_p, w_packed, bf_packed, layer, _row_tile(rp, 512))
        lf_t = jnp.swapaxes(omisc[:, M_LOGF:M_IW].reshape(b, lp, FOX_HEADS), 1, 2)
        c_t = _cumsum(lf_t)
        c_col = jnp.swapaxes(c_t, 1, 2)
        mb_p = _select_p(aiq, aik, omisc, b, lp, n_sel_p)
        mf_p, md_p = _attn_p(afq, afk, afv, adq, adkv, c_col, c_t, mb_p, b, lp)
        for k, a in enumerate(new_rows(ofk, ofv, odkv, omisc)):
            rows_p[k].append(a.reshape(b, lp, -1)[:, :l])

        ofk, ofv, odkv, omisc, afq, afk, afv, adq, adkv, aiq, aik = _proj(
            x_s, w_packed, bf_packed, layer, _row_tile(rs, 512))
        smp = lambda a: a.reshape(db, t_new, a.shape[1]).astype(F32)
        misc_s = smp(omisc)
        mb_s = _select_s(page_table, cik, smp(aiq), misc_s, layer, n_sel_s, group_s)
        lf_s = jnp.pad(jnp.swapaxes(misc_s[:, :, M_LOGF:M_IW], 1, 2), ((0, 0), (0, 0), (0, LANE - t_new)))
        mf_s, md_s = _attn_s(page_table, cfk, cfv, clf_t, cdk, cdv, smp(afq), smp(adq), smp(ofk), smp(ofv),
                             smp(odkv), lf_s, mb_s, layer)
        for k, a in enumerate(new_rows(ofk, ofv, odkv, omisc)):
            rows_s[k].append(a.reshape(db, t_new, -1))

        x_p = ffn(x_p, mf_p, md_p, layer)
        x_s = ffn(x_s, mf_s.reshape(rs, FOX_WIDTH).astype(BF16), md_s.reshape(rs, DSA_WIDTH).astype(BF16), layer)

    y_prompt = x_p.reshape(b, lp, d)[:, N_META:l]
    y_sample = x_s.reshape(db, t_new, d)
    shapes = [(FOX_HEADS, HEAD_DIM), (FOX_HEADS, HEAD_DIM), (FOX_HEADS,), (DSA_KV_HEADS, HEAD_DIM),
              (DSA_KV_HEADS, HEAD_DIM), (IDX_DIM,)]
    outs_p = [jnp.stack(r).reshape((depth, b, l) + s) for r, s in zip(rows_p, shapes)]
    outs_s = [jnp.stack(r).reshape((depth, db, t_new) + s) for r, s in zip(rows_s, shapes)]
    return (y_prompt, y_sample, *outs_p, *outs_s)
```
